```python
import math
import jax, jax.numpy as jnp
from jax import lax
import numpy as np

D_MODEL = 1024
BATCH = 2
SEQ = 8192
DEPTH = 1

HEAD_DIM = 64
FOX_HEADS = 8
SB_HEADS = 8
FOX_WIDTH = FOX_HEADS * HEAD_DIM
SB_WIDTH = SB_HEADS * HEAD_DIM
MIX_WIDTH = FOX_WIDTH + SB_WIDTH
IN_COLS = 3 * FOX_WIDTH + FOX_HEADS + 3 * SB_WIDTH
Q_BLOCK = 128
FORGET_BIAS_INIT = 3.0
N_EXPERTS = 32
TOP_K = 4
D_FF = D_MODEL
SWIGLU_LIMIT = 7.0
SWIGLU_ALPHA = 1.702
EXPERT_BLOCK = 128
NORM_EPS = 1e-5

kernel_name = "hybrid_fox_stickbreak_moe_layer"


def rmsnorm(x, g):
    xf = x.astype(jnp.float32)
    y = xf * lax.rsqrt(jnp.mean(xf * xf, axis=-1, keepdims=True) + NORM_EPS)
    return (y * g.astype(jnp.float32)).astype(x.dtype)


def to_query_blocks(t):
    b, s, h, d = t.shape
    return t.reshape(b, s // Q_BLOCK, Q_BLOCK, h, d).transpose(1, 0, 2, 3, 4)


def from_query_blocks(t):
    nb, b, qb, h, d = t.shape
    return t.transpose(1, 0, 2, 3, 4).reshape(b, nb * qb, h * d)


def forgetting_attention(q, k, v, log_f):
    b, s, h, d = q.shape
    scale = 1.0 / math.sqrt(d)
    c = jnp.cumsum(log_f.astype(jnp.float32), axis=1).transpose(0, 2, 1)
    nb = s // Q_BLOCK
    c_q = c.reshape(b, h, nb, Q_BLOCK).transpose(2, 0, 1, 3)
    kpos = jnp.arange(s)

    def block(args):
        qb, cqb, i = args
        logits = jnp.einsum('bqhd,bkhd->bhqk', qb, k).astype(jnp.float32) * scale
        logits = logits + cqb[..., None] - c[:, :, None, :]
        qpos = i * Q_BLOCK + jnp.arange(Q_BLOCK)
        causal = kpos[None, :] <= qpos[:, None]
        logits = jnp.where(causal, logits, -jnp.inf)
        p = jax.nn.softmax(logits, axis=-1).astype(v.dtype)
        return jnp.einsum('bhqk,bkhd->bqhd', p, v)

    out = lax.map(block, (to_query_blocks(q), c_q, jnp.arange(nb)))
    return from_query_blocks(out)


def stick_breaking_attention(q, k, v):
    b, s, h, d = q.shape
    scale = 1.0 / math.sqrt(d)
    nb = s // Q_BLOCK
    kpos = jnp.arange(s)

    def block(args):
        qb, i = args
        z = jnp.einsum('bqhd,bkhd->bhqk', qb, k).astype(jnp.float32) * scale
        qpos = i * Q_BLOCK + jnp.arange(Q_BLOCK)
        strict = kpos[None, :] < qpos[:, None]
        log_beta = jax.nn.log_sigmoid(z)
        log_1m = jnp.where(strict, jax.nn.log_sigmoid(-z), 0.0)
        suffix = lax.cumsum(log_1m, axis=3, reverse=True) - log_1m
        a = jnp.where(strict, jnp.exp(log_beta + suffix), 0.0).astype(v.dtype)
        return jnp.einsum('bhqk,bkhd->bqhd', a, v)

    out = lax.map(block, (to_query_blocks(q), jnp.arange(nb)))
    return from_query_blocks(out)


def moe_ffn(h, w_router, b_router, w_gate_up, b_gate_up, w_down, b_down):
    t, d = h.shape
    n_assign = t * TOP_K
    logits = (h @ w_router).astype(jnp.float32) + b_router.astype(jnp.float32)
    top_vals, top_idx = lax.top_k(logits, TOP_K)
    gates = jax.nn.softmax(top_vals, axis=-1).astype(h.dtype)

    flat_e = top_idx.reshape(-1)
    flat_tok = jnp.arange(n_assign, dtype=jnp.int32) // TOP_K
    flat_w = gates.reshape(-1)
    order = jnp.argsort(flat_e)
    sorted_e = flat_e[order]
    sorted_tok = flat_tok[order]
    sorted_w = flat_w[order]

    counts = jnp.bincount(flat_e, length=N_EXPERTS)
    starts = jnp.cumsum(counts) - counts
    padded_counts = (counts + EXPERT_BLOCK - 1) // EXPERT_BLOCK * EXPERT_BLOCK
    padded_ends = jnp.cumsum(padded_counts)
    padded_starts = padded_ends - padded_counts
    rank = jnp.arange(n_assign) - starts[sorted_e]
    dest = padded_starts[sorted_e] + rank

    n_pad = (n_assign + N_EXPERTS * (EXPERT_BLOCK - 1) + EXPERT_BLOCK - 1) // EXPERT_BLOCK * EXPERT_BLOCK
    n_blocks = n_pad // EXPERT_BLOCK
    row_tok = jnp.full((n_pad,), t, dtype=jnp.int32).at[dest].set(sorted_tok)
    row_w = jnp.zeros((n_pad,), h.dtype).at[dest].set(sorted_w)
    block_start = jnp.arange(n_blocks) * EXPERT_BLOCK
    block_e = jnp.minimum(jnp.searchsorted(padded_ends, block_start, side='right'), N_EXPERTS - 1)

    h_pad = jnp.concatenate([h, jnp.zeros((1, d), h.dtype)], axis=0)
    xs = h_pad[row_tok].reshape(n_blocks, EXPERT_BLOCK, d)

    def expert_block(args):
        xb, e = args
        gu = xb @ w_gate_up[e] + b_gate_up[e]
        gate = jnp.minimum(gu[:, 0::2], SWIGLU_LIMIT)
        up = jnp.clip(gu[:, 1::2], -SWIGLU_LIMIT, SWIGLU_LIMIT)
        act = gate * jax.nn.sigmoid(SWIGLU_ALPHA * gate) * (up + 1.0)
        return act @ w_down[e] + b_down[e]

    ys = lax.map(expert_block, (xs, block_e)).reshape(n_pad, d)
    out = jax.ops.segment_sum(ys * row_w[:, None], row_tok, num_segments=t + 1)
    return out[:t]


def setup_inputs(seed: int = 0) -> dict:
    key = jax.random.key(seed)
    ks = jax.random.split(key, 16)
    f32 = jnp.float32
    nrm = lambda k, shp, s: jax.random.normal(k, shp, f32) * s
    return {
        "x": nrm(ks[0], (BATCH, SEQ, D_MODEL), 1.0),
        "attn_norm_g": 1.0 + nrm(ks[1], (D_MODEL,), 0.02),
        "w_in": nrm(ks[2], (D_MODEL, IN_COLS), D_MODEL ** -0.5),
        "b_forget": FORGET_BIAS_INIT + nrm(ks[3], (FOX_HEADS,), 0.1),
        "fox_out_g": 1.0 + nrm(ks[4], (FOX_WIDTH,), 0.02),
        "sb_out_g": 1.0 + nrm(ks[5], (SB_WIDTH,), 0.02),
        "w_out": nrm(ks[6], (MIX_WIDTH, D_MODEL), MIX_WIDTH ** -0.5),
        "ffn_norm_g": 1.0 + nrm(ks[7], (D_MODEL,), 0.02),
        "w_router": nrm(ks[8], (D_MODEL, N_EXPERTS), D_MODEL ** -0.5),
        "b_router": nrm(ks[9], (N_EXPERTS,), 0.01),
        "w_gate_up": nrm(ks[10], (N_EXPERTS, D_MODEL, 2 * D_FF), D_MODEL ** -0.5),
        "b_gate_up": nrm(ks[11], (N_EXPERTS, 2 * D_FF), 0.01),
        "w_down": nrm(ks[12], (N_EXPERTS, D_FF, D_MODEL), D_FF ** -0.5),
        "b_down": nrm(ks[13], (N_EXPERTS, D_MODEL), 0.01),
        "final_norm_g": 1.0 + nrm(ks[14], (D_MODEL,), 0.02),
    }


def reference(x, attn_norm_g, w_in, b_forget, fox_out_g, sb_out_g, w_out, ffn_norm_g,
              w_router, b_router, w_gate_up, b_gate_up, w_down, b_down, final_norm_g):
    b, s, d = x.shape
    for _ in range(DEPTH):
        h = rmsnorm(x, attn_norm_g)
        proj = h @ w_in
        o = 0
        q_a = proj[..., o:o + FOX_WIDTH].reshape(b, s, FOX_HEADS, HEAD_DIM); o += FOX_WIDTH
        k_a = proj[..., o:o + FOX_WIDTH].reshape(b, s, FOX_HEADS, HEAD_DIM); o += FOX_WIDTH
        v_a = proj[..., o:o + FOX_WIDTH].reshape(b, s, FOX_HEADS, HEAD_DIM); o += FOX_WIDTH
        f_logit = proj[..., o:o + FOX_HEADS]; o += FOX_HEADS
        q_b = proj[..., o:o + SB_WIDTH].reshape(b, s, SB_HEADS, HEAD_DIM); o += SB_WIDTH
        k_b = proj[..., o:o + SB_WIDTH].reshape(b, s, SB_HEADS, HEAD_DIM); o += SB_WIDTH
        v_b = proj[..., o:o + SB_WIDTH].reshape(b, s, SB_HEADS, HEAD_DIM)

        log_f = jax.nn.log_sigmoid((f_logit + b_forget).astype(jnp.float32))
        y_a = rmsnorm(forgetting_attention(q_a, k_a, v_a, log_f), fox_out_g)
        y_b = rmsnorm(stick_breaking_attention(q_b, k_b, v_b), sb_out_g)
        x = x + jnp.concatenate([y_a, y_b], axis=-1) @ w_out

        h2 = rmsnorm(x, ffn_norm_g).reshape(b * s, d)
        x = x + moe_ffn(h2, w_router, b_router, w_gate_up, b_gate_up, w_down, b_down).reshape(b, s, d)
    return rmsnorm(x, final_norm_g)
```

```python
import functools
import math

import jax
import jax.numpy as jnp
from jax import lax
from jax.experimental import pallas as pl
from jax.experimental.pallas import tpu as pltpu

F32 = jnp.float32
BF16 = jnp.bfloat16

HEAD_DIM = 64
N_HEADS = 8
PAIR = 2 * HEAD_DIM
N_EXPERTS = 32
TOP_K = 4
LANES = 128
NORM_EPS = 1e-5
SWIGLU_LIMIT = 7.0
SWIGLU_ALPHA = 1.702
LOG2E = math.log2(math.e)
NEG_BIG = -1e30

TM_PROJ = 512
BQ = 256
BK = 256
TM_ROUTE = 256
TM_DISPATCH = 256
TM_COMBINE = 256
BM_MOE = 256
VMEM_LIMIT = 56 * 1024 * 1024


def _cparams(sem):
    return pltpu.CompilerParams(dimension_semantics=sem, vmem_limit_bytes=VMEM_LIMIT)


def _inproj_kernel(x_ref, g_ref, w_ref, wft_ref, bf_ref, tri_ref,
                   qa_ref, ka_ref, va_ref, qb_ref, kb_ref, vb_ref, bias_ref, carry_ref,
                   *, tiles_per_seq):
    i = pl.program_id(0)

    @pl.when(i % tiles_per_seq == 0)
    def _():
        carry_ref[...] = jnp.zeros_like(carry_ref)

    x = x_ref[...]
    ms = jnp.mean(x * x, axis=-1, keepdims=True)
    h = x * lax.rsqrt(ms + NORM_EPS) * g_ref[...]
    hb = h.astype(BF16)
    width = qa_ref.shape[1]
    for n, ref in enumerate((qa_ref, ka_ref, va_ref, qb_ref, kb_ref, vb_ref)):
        ref[...] = jnp.dot(hb, w_ref[:, n * width:(n + 1) * width],
                           preferred_element_type=F32).astype(BF16)

    ft = lax.dot_general(wft_ref[...], h, (((1,), (1,)), ((), ())),
                         precision=lax.Precision.HIGHEST, preferred_element_type=F32)
    ft = ft + bf_ref[:, 0:1]
    logf = jnp.minimum(ft, 0.0) - jnp.log(1.0 + jnp.exp(-jnp.abs(ft)))
    c = jnp.dot(logf, tri_ref[...], precision=lax.Precision.HIGHEST,
                preferred_element_type=F32) + carry_ref[:, 0:1]
    tm = c.shape[1]
    carry_ref[...] = jnp.broadcast_to(c[:, tm - 1:tm], carry_ref.shape)
    bias_ref[0] = -LOG2E * c


def _inproj(x2d, g, w_all, wft, bf, seq):
    t, d = x2d.shape
    tm = TM_PROJ
    width = w_all.shape[1] // 6
    tiles_per_seq = seq // tm
    nb = t // seq
    tri = (jnp.arange(tm)[:, None] <= jnp.arange(tm)[None, :]).astype(F32)
    out_bf = jax.ShapeDtypeStruct((t, width), BF16)
    return pl.pallas_call(
        functools.partial(_inproj_kernel, tiles_per_seq=tiles_per_seq),
        grid=(t // tm,),
        in_specs=[
            pl.BlockSpec((tm, d), lambda i: (i, 0)),
            pl.BlockSpec((1, d), lambda i: (0, 0)),
            pl.BlockSpec(w_all.shape, lambda i: (0, 0)),
            pl.BlockSpec(wft.shape, lambda i: (0, 0)),
            pl.BlockSpec(bf.shape, lambda i: (0, 0)),
            pl.BlockSpec((tm, tm), lambda i: (0, 0)),
        ],
        out_specs=[pl.BlockSpec((tm, width), lambda i: (i, 0))] * 6 + [
            pl.BlockSpec((1, N_HEADS, tm), lambda i: (i // tiles_per_seq, 0, i % tiles_per_seq)),
        ],
        out_shape=[out_bf] * 6 + [jax.ShapeDtypeStruct((nb, N_HEADS, seq), F32)],
        scratch_shapes=[pltpu.VMEM((N_HEADS, LANES), F32)],
        compiler_params=_cparams(("arbitrary",)),
        name="inproj",
    )(x2d, g, w_all, wft, bf, tri)


def _nt_dot(a, b):
    return lax.dot_general(a, b, (((1,), (1,)), ((), ())), preferred_element_type=F32)


def _fox_kernel(q_ref, k_ref, v_ref, bias_ref, o_ref):
    i = pl.program_id(2)
    bq = q_ref.shape[0]
    q2 = q_ref[...]
    lane_q = lax.broadcasted_iota(jnp.int32, q2.shape, 1)
    q_heads = (jnp.where(lane_q < HEAD_DIM, q2, jnp.zeros_like(q2)),
               jnp.where(lane_q >= HEAD_DIM, q2, jnp.zeros_like(q2)))
    row = lax.broadcasted_iota(jnp.int32, (bq, BK), 0)
    col = lax.broadcasted_iota(jnp.int32, (bq, BK), 1)
    causal = col <= row

    def step(j, carry, diag):
        start = pl.multiple_of(j * BK, BK)
        ks = k_ref[pl.ds(start, BK), :]
        vs = v_ref[pl.ds(start, BK), :]
        lane_k = lax.broadcasted_iota(jnp.int32, vs.shape, 1)
        v_heads = (jnp.where(lane_k < HEAD_DIM, vs, jnp.zeros_like(vs)),
                   jnp.where(lane_k >= HEAD_DIM, vs, jnp.zeros_like(vs)))
        new = []
        for hh in range(2):
            m, l, acc = carry[hh]
            s = _nt_dot(q_heads[hh], ks) + bias_ref[0, 0, hh:hh + 1, pl.ds(start, BK)]
            if diag:
                s = jnp.where(causal, s, NEG_BIG)
            m_new = jnp.maximum(m, jnp.max(s, axis=1, keepdims=True))
            p = jnp.exp2(s - m_new)
            alpha = jnp.exp2(m - m_new)
            l = alpha * l + jnp.sum(p, axis=1, keepdims=True)
            acc = alpha * acc + jnp.dot(p.astype(BF16), v_heads[hh], preferred_element_type=F32)
            new.append((m_new, l, acc))
        return tuple(new)

    init = tuple((jnp.full((bq, 1), NEG_BIG, F32), jnp.zeros((bq, 1), F32),
                  jnp.zeros((bq, PAIR), F32)) for _ in range(2))
    carry = lax.fori_loop(0, i, lambda j, c: step(j, c, False), init)
    (_, l_a, acc_a), (_, l_b, acc_b) = step(i, carry, True)
    o_ref[...] = acc_a / l_a + acc_b / l_b


def _attention_specs(seq, nq):
    q_spec = pl.BlockSpec((BQ, PAIR), lambda b, p, i: (b * nq + i, p))
    kv_spec = pl.BlockSpec((seq, PAIR), lambda b, p, i: (b, p))
    return q_spec, kv_spec


def _fox(q, k, v, bias, seq):
    t, width = q.shape
    nb, npair, nq = t // seq, width // PAIR, seq // BQ
    q_spec, kv_spec = _attention_specs(seq, nq)
    return pl.pallas_call(
        _fox_kernel,
        grid=(nb, npair, nq),
        in_specs=[q_spec, kv_spec, kv_spec,
                  pl.BlockSpec((1, 1, 2, seq), lambda b, p, i: (b, p, 0, 0))],
        out_specs=q_spec,
        out_shape=jax.ShapeDtypeStruct((t, width), F32),
        compiler_params=_cparams(("parallel", "parallel", "arbitrary")),
        name="fox",
    )(q, k, v, bias)


def _softplus(z):
    return jnp.maximum(z, 0.0) + jnp.log(1.0 + jnp.exp(-jnp.abs(z)))


def _sb_kernel(q_ref, k_ref, v_ref, tri_ref, o_ref):
    i = pl.program_id(2)
    bq = q_ref.shape[0]
    q2 = q_ref[...]
    lane_q = lax.broadcasted_iota(jnp.int32, q2.shape, 1)
    q_heads = (jnp.where(lane_q < HEAD_DIM, q2, jnp.zeros_like(q2)),
               jnp.where(lane_q >= HEAD_DIM, q2, jnp.zeros_like(q2)))
    row = lax.broadcasted_iota(jnp.int32, (bq, BK), 0)
    col = lax.broadcasted_iota(jnp.int32, (bq, BK), 1)
    strict = col < row
    tri = tri_ref[...]

    def step(j, carry, diag):
        start = pl.multiple_of(j * BK, BK)
        ks = k_ref[pl.ds(start, BK), :]
        vs = v_ref[pl.ds(start, BK), :]
        lane_k = lax.broadcasted_iota(jnp.int32, vs.shape, 1)
        v_heads = (jnp.where(lane_k < HEAD_DIM, vs, jnp.zeros_like(vs)),
                   jnp.where(lane_k >= HEAD_DIM, vs, jnp.zeros_like(vs)))
        new = []
        for hh in range(2):
            later, acc = carry[hh]
            z = _nt_dot(q_heads[hh], ks)
            sp = _softplus(z)
            if diag:
                sp = jnp.where(strict, sp, 0.0)
            hi = sp.astype(BF16)
            lo = (sp - hi.astype(F32)).astype(BF16)
            g = (jnp.dot(hi, tri, preferred_element_type=F32)
                 + jnp.dot(lo, tri, preferred_element_type=F32))
            a = jnp.exp(z - g - later)
            if diag:
                a = jnp.where(strict, a, 0.0)
            acc = acc + jnp.dot(a.astype(BF16), v_heads[hh], preferred_element_type=F32)
            new.append((later + g[:, 0:1], acc))
        return tuple(new)

    init = tuple((jnp.zeros((bq, 1), F32), jnp.zeros((bq, PAIR), F32)) for _ in range(2))
    carry = step(i, init, True)
    carry = lax.fori_loop(0, i, lambda n, c: step(i - 1 - n, c, False), carry)
    o_ref[...] = carry[0][1] + carry[1][1]


def _sb(q, k, v, seq):
    t, width = q.shape
    nb, npair, nq = t // seq, width // PAIR, seq // BQ
    q_spec, kv_spec = _attention_specs(seq, nq)
    tri = (jnp.arange(BK)[:, None] >= jnp.arange(BK)[None, :]).astype(BF16)
    return pl.pallas_call(
        _sb_kernel,
        grid=(nb, npair, nq),
        in_specs=[q_spec, kv_spec, kv_spec, pl.BlockSpec((BK, BK), lambda b, p, i: (0, 0))],
        out_specs=q_spec,
        out_shape=jax.ShapeDtypeStruct((t, width), F32),
        compiler_params=_cparams(("parallel", "parallel", "arbitrary")),
        name="sb",
    )(q, k, v, tri)


def _rms(y, g):
    return y * lax.rsqrt(jnp.mean(y * y, axis=-1, keepdims=True) + NORM_EPS) * g


def _outproj_kernel(x_ref, ya_ref, yb_ref, ga_ref, gb_ref, wo_ref, gf_ref, wr_ref, br_ref, ltri_ref,
                    x2_ref, h2_ref, meta_ref, cnt_ref, carry_ref):
    i = pl.program_id(0)

    @pl.when(i == 0)
    def _():
        carry_ref[...] = jnp.zeros_like(carry_ref)

    ya = _rms(ya_ref[...], ga_ref[...]).astype(BF16)
    yb = _rms(yb_ref[...], gb_ref[...]).astype(BF16)
    wa = ya_ref.shape[1]
    mix = (jnp.dot(ya, wo_ref[0:wa, :], preferred_element_type=F32)
           + jnp.dot(yb, wo_ref[wa:, :], preferred_element_type=F32))
    x2 = x_ref[...] + mix
    x2_ref[...] = x2
    h2 = _rms(x2, gf_ref[...])
    h2_ref[...] = h2

    logits = jnp.dot(h2, wr_ref[...], precision=lax.Precision.HIGHEST,
                     preferred_element_type=F32) + br_ref[...]
    tm = logits.shape[0]
    lane = lax.broadcasted_iota(jnp.int32, (tm, LANES), 1)
    lane_f = lane.astype(F32)
    work = logits
    vals, sels, idxs = [], [], []
    for _ in range(TOP_K):
        mx = jnp.max(work, axis=1, keepdims=True)
        idx = jnp.min(jnp.where(work == mx, lane_f, float(LANES)), axis=1, keepdims=True)
        sel = lane_f == idx
        vals.append(mx)
        idxs.append(idx)
        sels.append(sel)
        work = jnp.where(sel, -jnp.inf, work)
    exps = [jnp.exp(v - vals[0]) for v in vals]
    denom = exps[0] + exps[1] + exps[2] + exps[3]
    gates = [e / denom for e in exps]

    onehot = jnp.zeros((tm, LANES), F32)
    for sel in sels:
        onehot = onehot + jnp.where(sel, 1.0, 0.0)
    before = jnp.dot(ltri_ref[...], onehot.astype(BF16), preferred_element_type=F32) + carry_ref[0:1, :]
    meta = jnp.zeros((tm, LANES), F32)
    for k in range(TOP_K):
        rank = jnp.sum(jnp.where(sels[k], before, 0.0), axis=1, keepdims=True)
        meta = jnp.where(lane == k, idxs[k], meta)
        meta = jnp.where(lane == TOP_K + k, rank, meta)
        meta = jnp.where(lane == 2 * TOP_K + k, gates[k], meta)
    meta_ref[...] = meta
    total = carry_ref[0:1, :] + jnp.sum(onehot, axis=0, keepdims=True)
    carry_ref[...] = jnp.broadcast_to(total, carry_ref.shape)
    cnt_ref[...] = jnp.broadcast_to(total, cnt_ref.shape)


def _outproj(x2d, ya, yb, ga, gb, wo, gf, wr, br):
    t, d = x2d.shape
    tm = TM_ROUTE
    wa = ya.shape[1]
    ltri = (jnp.arange(tm)[:, None] > jnp.arange(tm)[None, :]).astype(BF16)
    tile = lambda w: pl.BlockSpec((tm, w), lambda i: (i, 0))
    whole = lambda a: pl.BlockSpec(a.shape, lambda i: (0, 0))
    return pl.pallas_call(
        _outproj_kernel,
        grid=(t // tm,),
        in_specs=[tile(d), tile(wa), tile(wa), whole(ga), whole(gb), whole(wo), whole(gf),
                  whole(wr), whole(br), whole(ltri)],
        out_specs=[tile(d), tile(d), tile(LANES), pl.BlockSpec((8, LANES), lambda i: (0, 0))],
        out_shape=[jax.ShapeDtypeStruct((t, d), F32), jax.ShapeDtypeStruct((t, d), F32),
                   jax.ShapeDtypeStruct((t, LANES), F32), jax.ShapeDtypeStruct((8, LANES), F32)],
        scratch_shapes=[pltpu.VMEM((8, LANES), F32)],
        compiler_params=_cparams(("arbitrary",)),
        name="outproj",
    )(x2d, ya, yb, ga, gb, wo, gf, wr, br, ltri)


def _row_copy(src, src_row, dst, dst_row, sem):
    return pltpu.make_async_copy(src.at[pl.ds(src_row, 1), :], dst.at[pl.ds(dst_row, 1), :], sem)


def _dispatch_kernel(dest_ref, h_ref, xs_init_ref, xs_ref, sem):
    del xs_init_ref
    tm = h_ref.shape[0]

    def start(t, _):
        for k in range(TOP_K):
            _row_copy(h_ref, t, xs_ref, dest_ref[t * TOP_K + k], sem).start()
        return 0

    def wait(t, _):
        for k in range(TOP_K):
            _row_copy(h_ref, t, xs_ref, dest_ref[t * TOP_K + k], sem).wait()
        return 0

    lax.fori_loop(0, tm, start, 0)
    lax.fori_loop(0, tm, wait, 0)


def _dispatch(dest_flat, h2, n_rows):
    t, d = h2.shape
    tm = TM_DISPATCH
    xs_init = jnp.zeros((n_rows, d), h2.dtype)
    return pl.pallas_call(
        _dispatch_kernel,
        grid=(t // tm,),
        in_specs=[
            pl.BlockSpec((tm * TOP_K,), lambda i: (i,), memory_space=pltpu.SMEM),
            pl.BlockSpec((tm, d), lambda i: (i, 0)),
            pl.BlockSpec(memory_space=pl.ANY),
        ],
        out_specs=pl.BlockSpec(memory_space=pl.ANY),
        out_shape=jax.ShapeDtypeStruct((n_rows, d), h2.dtype),
        scratch_shapes=[pltpu.SemaphoreType.DMA],
        input_output_aliases={2: 0},
        compiler_params=_cparams(("arbitrary",)),
        name="dispatch",
    )(dest_flat, h2, xs_init)


def _moe_kernel(be_ref, nv_ref, xs_ref, wg_ref, wu_ref, wd_ref, bg_ref, bu_ref, bd_ref, ys_ref):
    i = pl.program_id(0)

    @pl.when(i < nv_ref[0])
    def _():
        xb = xs_ref[...].astype(BF16)
        gate = jnp.dot(xb, wg_ref[0], preferred_element_type=F32) + bg_ref[0]
        up = jnp.dot(xb, wu_ref[0], preferred_element_type=F32) + bu_ref[0]
        gate = jnp.minimum(gate, SWIGLU_LIMIT)
        up = jnp.clip(up, -SWIGLU_LIMIT, SWIGLU_LIMIT)
        act = gate * (1.0 / (1.0 + jnp.exp(-SWIGLU_ALPHA * gate))) * (up + 1.0)
        ys_ref[...] = jnp.dot(act.astype(BF16), wd_ref[0], preferred_element_type=F32) + bd_ref[0]

    @pl.when(i >= nv_ref[0])
    def _():
        ys_ref[...] = jnp.zeros_like(ys_ref)


def _moe(block_e, n_valid, xs, wg, wu, wd, bg, bu, bd):
    n_rows, d = xs.shape
    bm = BM_MOE
    dff = wg.shape[2]
    w_spec = lambda a: pl.BlockSpec((1,) + a.shape[1:], lambda i, be, nv: (be[i], 0, 0))
    grid_spec = pltpu.PrefetchScalarGridSpec(
        num_scalar_prefetch=2,
        grid=(n_rows // bm,),
        in_specs=[pl.BlockSpec((bm, d), lambda i, be, nv: (i, 0)),
                  w_spec(wg), w_spec(wu), w_spec(wd), w_spec(bg), w_spec(bu), w_spec(bd)],
        out_specs=pl.BlockSpec((bm, d), lambda i, be, nv: (i, 0)),
    )
    del dff
    return pl.pallas_call(
        _moe_kernel,
        grid_spec=grid_spec,
        out_shape=jax.ShapeDtypeStruct((n_rows, d), F32),
        compiler_params=_cparams(("arbitrary",)),
        name="moe",
    )(block_e, n_valid, xs, wg, wu, wd, bg, bu, bd)


def _combine_kernel(dest_ref, x2_ref, meta_ref, g_ref, ys_ref, o_ref, buf_ref, sem):
    tm = x2_ref.shape[0]

    def start(t, _):
        for k in range(TOP_K):
            _row_copy(ys_ref, dest_ref[t * TOP_K + k], buf_ref.at[k], t, sem).start()
        return 0

    def wait(t, _):
        for k in range(TOP_K):
            _row_copy(ys_ref, dest_ref[t * TOP_K + k], buf_ref.at[k], t, sem).wait()
        return 0

    lax.fori_loop(0, tm, start, 0)
    lax.fori_loop(0, tm, wait, 0)
    meta = meta_ref[...]
    y = x2_ref[...]
    for k in range(TOP_K):
        y = y + meta[:, 2 * TOP_K + k:2 * TOP_K + k + 1] * buf_ref[k]
    o_ref[...] = _rms(y, g_ref[...])


def _combine(dest_flat, x2, meta, g, ys):
    t, d = x2.shape
    tm = TM_COMBINE
    return pl.pallas_call(
        _combine_kernel,
        grid=(t // tm,),
        in_specs=[
            pl.BlockSpec((tm * TOP_K,), lambda i: (i,), memory_space=pltpu.SMEM),
            pl.BlockSpec((tm, d), lambda i: (i, 0)),
            pl.BlockSpec((tm, LANES), lambda i: (i, 0)),
            pl.BlockSpec((1, d), lambda i: (0, 0)),
            pl.BlockSpec(memory_space=pl.ANY),
        ],
        out_specs=pl.BlockSpec((tm, d), lambda i: (i, 0)),
        out_shape=jax.ShapeDtypeStruct((t, d), F32),
        scratch_shapes=[pltpu.VMEM((TOP_K, tm, d), F32), pltpu.SemaphoreType.DMA],
        compiler_params=_cparams(("arbitrary",)),
        name="combine",
    )(dest_flat, x2, meta, g, ys)


def kernel(x, attn_norm_g, w_in, b_forget, fox_out_g, sb_out_g, w_out, ffn_norm_g, w_router, b_router,
           w_gate_up, b_gate_up, w_down, b_down, final_norm_g):
    b, s, d = x.shape
    t = b * s
    fw = N_HEADS * HEAD_DIM
    x2d = x.reshape(t, d)

    o = 0
    parts = []
    for width in (fw, fw, fw, N_HEADS, fw, fw, fw):
        parts.append(w_in[:, o:o + width])
        o += width
    wqa, wka, wva, wf, wqb, wkb, wvb = parts
    scale = 1.0 / math.sqrt(HEAD_DIM)
    w_all = jnp.concatenate([wqa * (scale * LOG2E), wka, wva, wqb * scale, wkb, wvb], axis=1).astype(BF16)
    wft = wf.T
    bf = jnp.broadcast_to(b_forget.astype(F32)[:, None], (N_HEADS, LANES))

    qa, ka, va, qb, kb, vb, bias = _inproj(x2d, attn_norm_g.reshape(1, d), w_all, wft, bf, s)
    bias = bias.reshape(b, N_HEADS // 2, 2, s)
    ya = _fox(qa, ka, va, bias, s)
    yb = _sb(qb, kb, vb, s)

    wr = jnp.zeros((d, LANES), F32).at[:, :N_EXPERTS].set(w_router)
    br = jnp.full((1, LANES), NEG_BIG, F32).at[0, :N_EXPERTS].set(b_router)
    x2, h2, meta, cnt = _outproj(x2d, ya, yb, fox_out_g.reshape(1, fw), sb_out_g.reshape(1, fw),
                                 w_out.astype(BF16), ffn_norm_g.reshape(1, d), wr, br)

    bm = BM_MOE
    n_blocks = (t * TOP_K + N_EXPERTS * (bm - 1)) // bm
    counts = cnt[0, :N_EXPERTS].astype(jnp.int32)
    padded = (counts + bm - 1) // bm * bm
    ends = jnp.cumsum(padded)
    starts = ends - padded
    top_idx = meta[:, 0:TOP_K].astype(jnp.int32)
    rank = meta[:, TOP_K:2 * TOP_K].astype(jnp.int32)
    dest = (starts[top_idx] + rank).reshape(t * TOP_K)
    block_start = jnp.arange(n_blocks, dtype=jnp.int32) * bm
    block_e = jnp.minimum(jnp.searchsorted(ends, block_start, side="right"), N_EXPERTS - 1).astype(jnp.int32)
    n_valid = (ends[-1:] // bm).astype(jnp.int32)

    xs = _dispatch(dest, h2, n_blocks * bm)
    dff = w_down.shape[1]
    wg = w_gate_up[:, :, 0::2].astype(BF16)
    wu = w_gate_up[:, :, 1::2].astype(BF16)
    bg = b_gate_up[:, 0::2].reshape(N_EXPERTS, 1, dff)
    bu = b_gate_up[:, 1::2].reshape(N_EXPERTS, 1, dff)
    ys = _moe(block_e, n_valid, xs, wg, wu, w_down.astype(BF16), bg, bu, b_down.reshape(N_EXPERTS, 1, d))
    out = _combine(dest, x2, meta, final_norm_g.reshape(1, d), ys)
    return out.reshape(b, s, d)
```

```python
import functools
import math

import jax
import jax.numpy as jnp
from jax import lax
from jax.experimental import pallas as pl
from jax.experimental.pallas import tpu as pltpu

F32 = jnp.float32
BF16 = jnp.bfloat16

HEAD_DIM = 64
N_HEADS = 8
N_EXPERTS = 32
TOP_K = 4
LANES = 128
NORM_EPS = 1e-5
SWIGLU_LIMIT = 7.0
SWIGLU_ALPHA = 1.702
LOG2E = math.log2(math.e)
NEG_BIG = -1e30
N_BIAS_PARTS = 3

TM_PROJ = 512
BQ = 512
BK = 512
SUB = 256
TM_ROUTE = 256
TM_DISPATCH = 256
TM_COMBINE = 256
BM_MOE = 256
VMEM_LIMIT = 56 * 1024 * 1024


def _cparams(sem):
    return pltpu.CompilerParams(dimension_semantics=sem, vmem_limit_bytes=VMEM_LIMIT)


def _hdot(a, b):
    return jnp.dot(a, b, precision=lax.Precision.HIGHEST, preferred_element_type=F32)


def _bdot(a, b):
    return jnp.dot(a, b, preferred_element_type=F32)


def _nt_dot(a, b):
    return lax.dot_general(a, b, (((1,), (1,)), ((), ())), preferred_element_type=F32)


def _inproj_kernel(x_ref, g_ref, w_ref, wf_ref, bf_ref, tri_ref, place_ref,
                   qa_ref, ka_ref, va_ref, qb_ref, kb_ref, vb_ref, carry_ref, *, tiles_per_seq):
    i = pl.program_id(0)

    @pl.when(i % tiles_per_seq == 0)
    def _():
        carry_ref[...] = jnp.zeros_like(carry_ref)

    x = x_ref[...]
    ms = jnp.mean(x * x, axis=-1, keepdims=True)
    h = x * lax.rsqrt(ms + NORM_EPS) * g_ref[...]
    hb = h.astype(BF16)

    logit = _hdot(h, wf_ref[...]) + bf_ref[...]
    logf = jnp.minimum(logit, 0.0) - jnp.log(1.0 + jnp.exp(-jnp.abs(logit)))
    c = _hdot(tri_ref[...], logf) + carry_ref[0:1, :]
    tm = c.shape[0]
    carry_ref[...] = jnp.broadcast_to(c[tm - 1:tm, :], carry_ref.shape)
    rest = -LOG2E * c
    placed = jnp.zeros((tm, ka_ref.shape[1]), F32)
    for n in range(N_BIAS_PARTS):
        part = rest.astype(BF16)
        rest = rest - part.astype(F32)
        placed = placed + _bdot(part, place_ref[n])

    lane = lax.broadcasted_iota(jnp.int32, (1, qa_ref.shape[1]), 1) % LANES
    q_extra = jnp.where((lane >= HEAD_DIM) & (lane < HEAD_DIM + N_BIAS_PARTS), 1.0, 0.0)
    v_extra = jnp.where(lane >= HEAD_DIM, 1.0, 0.0)
    extras = (q_extra, placed, v_extra, None, None, None)
    offset = 0
    for ref, extra in zip((qa_ref, ka_ref, va_ref, qb_ref, kb_ref, vb_ref), extras):
        width = ref.shape[1]
        out = _bdot(hb, w_ref[:, offset:offset + width])
        ref[...] = (out if extra is None else out + extra).astype(BF16)
        offset += width


def _pad_heads(w):
    d = w.shape[0]
    w = w.reshape(d, N_HEADS, HEAD_DIM)
    return jnp.concatenate([w, jnp.zeros_like(w)], axis=2).reshape(d, N_HEADS * LANES)


def _inproj(x2d, g, w_all, wf, bf, seq):
    t, d = x2d.shape
    tm = TM_PROJ
    width = N_HEADS * LANES
    tiles_per_seq = seq // tm
    tri = (jnp.arange(tm)[:, None] >= jnp.arange(tm)[None, :]).astype(F32)
    src = jnp.arange(LANES)[:, None]
    dst = jnp.arange(width)[None, :]
    place = jnp.stack([((dst == src * LANES + HEAD_DIM + n) & (src < N_HEADS)).astype(BF16)
                       for n in range(N_BIAS_PARTS)])
    pair_width = N_HEADS * HEAD_DIM
    widths = (width, width, width, pair_width, pair_width, width)
    assert sum(widths) == w_all.shape[1]
    whole = lambda a: pl.BlockSpec(a.shape, lambda i: (0,) * a.ndim)
    return pl.pallas_call(
        functools.partial(_inproj_kernel, tiles_per_seq=tiles_per_seq),
        grid=(t // tm,),
        in_specs=[pl.BlockSpec((tm, d), lambda i: (i, 0)), whole(g), whole(w_all), whole(wf), whole(bf),
                  whole(tri), whole(place)],
        out_specs=[pl.BlockSpec((tm, w), lambda i: (i, 0)) for w in widths],
        out_shape=[jax.ShapeDtypeStruct((t, w), BF16) for w in widths],
        scratch_shapes=[pltpu.VMEM((8, LANES), F32)],
        compiler_params=_cparams(("arbitrary",)),
        name="inproj",
    )(x2d, g, w_all, wf, bf, tri, place)


def _pair_out(o_even, o_odd):
    lane = lax.broadcasted_iota(jnp.int32, o_even.shape, 1)
    return jnp.where(lane < HEAD_DIM, o_even, pltpu.roll(o_odd, HEAD_DIM, axis=1))


def _fox_kernel(q_ref, k_ref, v_ref, o_ref):
    i = pl.program_id(2)
    bq = q_ref.shape[0]
    q_heads = (q_ref[:, 0:LANES], q_ref[:, LANES:2 * LANES])
    row = lax.broadcasted_iota(jnp.int32, (bq, BK), 0)
    col = lax.broadcasted_iota(jnp.int32, (bq, BK), 1)
    causal = col <= row

    def step(j, carry, diag):
        start = pl.multiple_of(j * BK, BK)
        new = []
        for hh in range(2):
            m, acc = carry[hh]
            ks = k_ref[pl.ds(start, BK), hh * LANES:(hh + 1) * LANES]
            vs = v_ref[pl.ds(start, BK), hh * LANES:(hh + 1) * LANES]
            s = _nt_dot(q_heads[hh], ks)
            if diag:
                s = jnp.where(causal, s, NEG_BIG)
            m_new = jnp.maximum(m, jnp.max(s, axis=1, keepdims=True))
            p = jnp.exp2(s - m_new)
            acc = jnp.exp2(m - m_new) * acc + _bdot(p.astype(BF16), vs)
            new.append((m_new, acc))
        return tuple(new)

    init = tuple((jnp.full((bq, 1), NEG_BIG, F32), jnp.zeros((bq, LANES), F32)) for _ in range(2))
    carry = lax.fori_loop(0, i, lambda j, c: step(j, c, False), init)
    (_, acc_a), (_, acc_b) = step(i, carry, True)
    norm = lambda acc: acc / pltpu.roll(acc, HEAD_DIM, axis=1)
    o_ref[...] = _pair_out(norm(acc_a), norm(acc_b))


def _softplus(z):
    return jnp.maximum(z, 0.0) + jnp.log(1.0 + jnp.exp(-jnp.abs(z)))


def _sb_kernel(q_ref, k_ref, v_ref, tri_ref, o_ref):
    i = pl.program_id(2)
    bq = q_ref.shape[0]
    n_sub = bq // SUB
    q2 = q_ref[...]
    lane_q = lax.broadcasted_iota(jnp.int32, q2.shape, 1)
    q_heads = (jnp.where(lane_q < HEAD_DIM, q2, jnp.zeros_like(q2)),
               jnp.where(lane_q >= HEAD_DIM, q2, jnp.zeros_like(q2)))
    row = lax.broadcasted_iota(jnp.int32, (bq, SUB), 0)
    col = lax.broadcasted_iota(jnp.int32, (bq, SUB), 1)
    tri = tri_ref[...]

    def sub_step(start, carry, strict):
        ks = k_ref[pl.ds(start, SUB), :]
        new = []
        for hh in range(2):
            later, acc = carry[hh]
            vs = v_ref[pl.ds(start, SUB), hh * LANES:(hh + 1) * LANES]
            z = _nt_dot(q_heads[hh], ks)
            sp = _softplus(z)
            if strict is not None:
                sp = jnp.where(strict, sp, 0.0)
            hi = sp.astype(BF16)
            lo = (sp - hi.astype(F32)).astype(BF16)
            g = _bdot(hi, tri) + _bdot(lo, tri)
            a = jnp.exp(z - g - later)
            if strict is not None:
                a = jnp.where(strict, a, 0.0)
            acc = acc + _bdot(a.astype(BF16), vs)
            new.append((later + g[:, 0:1], acc))
        return tuple(new)

    carry = tuple((jnp.zeros((bq, 1), F32), jnp.zeros((bq, LANES), F32)) for _ in range(2))
    for u in reversed(range(n_sub)):
        carry = sub_step(pl.multiple_of(i * bq + u * SUB, SUB), carry, col + u * SUB < row)

    def block(n, c):
        base = (i - 1 - n) * bq
        for u in reversed(range(n_sub)):
            c = sub_step(pl.multiple_of(base + u * SUB, SUB), c, None)
        return c

    carry = lax.fori_loop(0, i, block, carry)
    o_ref[...] = _pair_out(carry[0][1], carry[1][1])


def _attention_call(body, name, q, k, v, extra, seq, q_lanes, k_lanes):
    t = q.shape[0]
    nb, nq, npair = t // seq, seq // BQ, N_HEADS // 2
    q_spec = lambda w: pl.BlockSpec((BQ, w), lambda b, p, i: (b * nq + i, p))
    kv_spec = lambda w: pl.BlockSpec((seq, w), lambda b, p, i: (b, p))
    return pl.pallas_call(
        body,
        grid=(nb, npair, nq),
        in_specs=[q_spec(q_lanes), kv_spec(k_lanes), kv_spec(2 * LANES)]
        + [pl.BlockSpec(a.shape, lambda b, p, i: (0,) * a.ndim) for a in extra],
        out_specs=q_spec(LANES),
        out_shape=jax.ShapeDtypeStruct((t, npair * LANES), F32),
        compiler_params=_cparams(("parallel", "parallel", "arbitrary")),
        name=name,
    )(q, k, v, *extra)


def _rms(y, g):
    return y * lax.rsqrt(jnp.mean(y * y, axis=-1, keepdims=True) + NORM_EPS) * g


def _outproj_kernel(x_ref, ya_ref, yb_ref, ga_ref, gb_ref, wo_ref, gf_ref, wr_ref, br_ref, ltri_ref,
                    x2_ref, h2_ref, meta_ref, cnt_ref, carry_ref):
    i = pl.program_id(0)

    @pl.when(i == 0)
    def _():
        carry_ref[...] = jnp.zeros_like(carry_ref)

    ya = _rms(ya_ref[...], ga_ref[...]).astype(BF16)
    yb = _rms(yb_ref[...], gb_ref[...]).astype(BF16)
    wa = ya_ref.shape[1]
    x2 = x_ref[...] + _bdot(ya, wo_ref[0:wa, :]) + _bdot(yb, wo_ref[wa:, :])
    x2_ref[...] = x2
    h2 = _rms(x2, gf_ref[...])
    h2_ref[...] = h2

    logits = _hdot(h2, wr_ref[...]) + br_ref[...]
    tm = logits.shape[0]
    lane = lax.broadcasted_iota(jnp.int32, (tm, LANES), 1)
    lane_f = lane.astype(F32)
    work = logits
    vals, sels, idxs = [], [], []
    for _ in range(TOP_K):
        mx = jnp.max(work, axis=1, keepdims=True)
        idx = jnp.min(jnp.where(work == mx, lane_f, float(LANES)), axis=1, keepdims=True)
        sel = lane_f == idx
        vals.append(mx)
        idxs.append(idx)
        sels.append(sel)
        work = jnp.where(sel, -jnp.inf, work)
    exps = [jnp.exp(v - vals[0]) for v in vals]
    denom = exps[0] + exps[1] + exps[2] + exps[3]
    gates = [e / denom for e in exps]

    onehot = jnp.zeros((tm, LANES), F32)
    for sel in sels:
        onehot = onehot + jnp.where(sel, 1.0, 0.0)
    before = _bdot(ltri_ref[...], onehot.astype(BF16)) + carry_ref[0:1, :]
    meta = jnp.zeros((tm, LANES), F32)
    for k in range(TOP_K):
        rank = jnp.sum(jnp.where(sels[k], before, 0.0), axis=1, keepdims=True)
        meta = jnp.where(lane == k, idxs[k], meta)
        meta = jnp.where(lane == TOP_K + k, rank, meta)
        meta = jnp.where(lane == 2 * TOP_K + k, gates[k], meta)
    meta_ref[...] = meta
    total = carry_ref[0:1, :] + jnp.sum(onehot, axis=0, keepdims=True)
    carry_ref[...] = jnp.broadcast_to(total, carry_ref.shape)
    cnt_ref[...] = jnp.broadcast_to(total, cnt_ref.shape)


def _outproj(x2d, ya, yb, ga, gb, wo, gf, wr, br):
    t, d = x2d.shape
    tm = TM_ROUTE
    wa = ya.shape[1]
    ltri = (jnp.arange(tm)[:, None] > jnp.arange(tm)[None, :]).astype(BF16)
    tile = lambda w: pl.BlockSpec((tm, w), lambda i: (i, 0))
    whole = lambda a: pl.BlockSpec(a.shape, lambda i: (0, 0))
    return pl.pallas_call(
        _outproj_kernel,
        grid=(t // tm,),
        in_specs=[tile(d), tile(wa), tile(wa), whole(ga), whole(gb), whole(wo), whole(gf),
                  whole(wr), whole(br), whole(ltri)],
        out_specs=[tile(d), tile(d), tile(LANES), pl.BlockSpec((8, LANES), lambda i: (0, 0))],
        out_shape=[jax.ShapeDtypeStruct((t, d), F32), jax.ShapeDtypeStruct((t, d), F32),
                   jax.ShapeDtypeStruct((t, LANES), F32), jax.ShapeDtypeStruct((8, LANES), F32)],
        scratch_shapes=[pltpu.VMEM((8, LANES), F32)],
        compiler_params=_cparams(("arbitrary",)),
        name="outproj",
    )(x2d, ya, yb, ga, gb, wo, gf, wr, br, ltri)


def _row_copy(src, src_row, dst, dst_row, sem):
    return pltpu.make_async_copy(src.at[pl.ds(src_row, 1), :], dst.at[pl.ds(dst_row, 1), :], sem)


def _dispatch_kernel(dest_ref, h_ref, xs_init_ref, xs_ref, sem):
    del xs_init_ref
    tm = h_ref.shape[0]

    def start(t, _):
        for k in range(TOP_K):
            _row_copy(h_ref, t, xs_ref, dest_ref[t * TOP_K + k], sem).start()
        return 0

    def wait(t, _):
        for k in range(TOP_K):
            _row_copy(h_ref, t, xs_ref, dest_ref[t * TOP_K + k], sem).wait()
        return 0

    lax.fori_loop(0, tm, start, 0)
    lax.fori_loop(0, tm, wait, 0)


def _dispatch(dest_flat, h2, n_rows):
    t, d = h2.shape
    tm = TM_DISPATCH
    xs_init = jnp.zeros((n_rows, d), h2.dtype)
    return pl.pallas_call(
        _dispatch_kernel,
        grid=(t // tm,),
        in_specs=[
            pl.BlockSpec((tm * TOP_K,), lambda i: (i,), memory_space=pltpu.SMEM),
            pl.BlockSpec((tm, d), lambda i: (i, 0)),
            pl.BlockSpec(memory_space=pl.ANY),
        ],
        out_specs=pl.BlockSpec(memory_space=pl.ANY),
        out_shape=jax.ShapeDtypeStruct((n_rows, d), h2.dtype),
        scratch_shapes=[pltpu.SemaphoreType.DMA],
        input_output_aliases={2: 0},
        compiler_params=_cparams(("arbitrary",)),
        name="dispatch",
    )(dest_flat, h2, xs_init)


def _moe_kernel(be_ref, first_ref, nv_ref, xs_ref, wgu_ref, wd_ref, bgu_ref, bd_ref, ys_ref,
                wgu_bf_ref, wd2_ref):
    del be_ref
    i = pl.program_id(0)

    @pl.when(first_ref[i] == 1)
    def _():
        wgu_bf_ref[...] = wgu_ref[0].astype(BF16)
        wd = wd_ref[0].astype(BF16).astype(F32)
        bits = pltpu.bitcast(wd, jnp.uint32)
        wd2_ref[...] = bits | (bits >> 16)

    @pl.when(i < nv_ref[0])
    def _():
        xb = xs_ref[...].astype(BF16)
        gu = _bdot(xb, wgu_bf_ref[...]) + bgu_ref[0]
        even = lax.broadcasted_iota(jnp.int32, (gu.shape[0], LANES), 1) % 2 == 0
        acts = []
        for c in range(gu.shape[1] // LANES):
            blk = gu[:, c * LANES:(c + 1) * LANES]
            gate = jnp.minimum(blk, SWIGLU_LIMIT)
            glu = gate * (1.0 / (1.0 + jnp.exp(-SWIGLU_ALPHA * gate)))
            up1 = jnp.clip(blk, -SWIGLU_LIMIT, SWIGLU_LIMIT) + 1.0
            acts.append(jnp.where(even, glu * pltpu.roll(up1, LANES - 1, axis=1), 0.0).astype(BF16))
        act = jnp.concatenate(acts, axis=1)
        wd2 = pltpu.bitcast(wd2_ref[...], BF16)
        ys_ref[...] = _bdot(act, wd2) + bd_ref[0]

    @pl.when(i >= nv_ref[0])
    def _():
        ys_ref[...] = jnp.zeros_like(ys_ref)


def _moe(block_e, first, n_valid, xs, wgu, wd, bgu, bd):
    n_rows, d = xs.shape
    bm = BM_MOE
    dff = wd.shape[1]
    w_spec = lambda a: pl.BlockSpec((1,) + a.shape[1:], lambda i, be, fi, nv: (be[i], 0, 0))
    grid_spec = pltpu.PrefetchScalarGridSpec(
        num_scalar_prefetch=3,
        grid=(n_rows // bm,),
        in_specs=[pl.BlockSpec((bm, d), lambda i, be, fi, nv: (i, 0)),
                  w_spec(wgu), w_spec(wd), w_spec(bgu), w_spec(bd)],
        out_specs=pl.BlockSpec((bm, d), lambda i, be, fi, nv: (i, 0)),
        scratch_shapes=[pltpu.VMEM((d, 2 * dff), BF16), pltpu.VMEM((dff, d), jnp.uint32)],
    )
    return pl.pallas_call(
        _moe_kernel,
        grid_spec=grid_spec,
        out_shape=jax.ShapeDtypeStruct((n_rows, d), F32),
        compiler_params=_cparams(("arbitrary",)),
        name="moe",
    )(block_e, first, n_valid, xs, wgu, wd, bgu, bd)


def _combine_kernel(dest_ref, x2_ref, meta_ref, g_ref, ys_ref, o_ref, buf_ref, sem):
    tm = x2_ref.shape[0]

    def start(t, _):
        for k in range(TOP_K):
            _row_copy(ys_ref, dest_ref[t * TOP_K + k], buf_ref.at[k], t, sem).start()
        return 0

    def wait(t, _):
        for k in range(TOP_K):
            _row_copy(ys_ref, dest_ref[t * TOP_K + k], buf_ref.at[k], t, sem).wait()
        return 0

    lax.fori_loop(0, tm, start, 0)
    lax.fori_loop(0, tm, wait, 0)
    meta = meta_ref[...]
    y = x2_ref[...]
    for k in range(TOP_K):
        y = y + meta[:, 2 * TOP_K + k:2 * TOP_K + k + 1] * buf_ref[k]
    o_ref[...] = _rms(y, g_ref[...])


def _combine(dest_flat, x2, meta, g, ys):
    t, d = x2.shape
    tm = TM_COMBINE
    return pl.pallas_call(
        _combine_kernel,
        grid=(t // tm,),
        in_specs=[
            pl.BlockSpec((tm * TOP_K,), lambda i: (i,), memory_space=pltpu.SMEM),
            pl.BlockSpec((tm, d), lambda i: (i, 0)),
            pl.BlockSpec((tm, LANES), lambda i: (i, 0)),
            pl.BlockSpec((1, d), lambda i: (0, 0)),
            pl.BlockSpec(memory_space=pl.ANY),
        ],
        out_specs=pl.BlockSpec((tm, d), lambda i: (i, 0)),
        out_shape=jax.ShapeDtypeStruct((t, d), F32),
        scratch_shapes=[pltpu.VMEM((TOP_K, tm, d), F32), pltpu.SemaphoreType.DMA],
        compiler_params=_cparams(("arbitrary",)),
        name="combine",
    )(dest_flat, x2, meta, g, ys)


def kernel(x, attn_norm_g, w_in, b_forget, fox_out_g, sb_out_g, w_out, ffn_norm_g, w_router, b_router,
           w_gate_up, b_gate_up, w_down, b_down, final_norm_g):
    b, s, d = x.shape
    t = b * s
    fw = N_HEADS * HEAD_DIM
    x2d = x.reshape(t, d)

    o = 0
    parts = []
    for width in (fw, fw, fw, N_HEADS, fw, fw, fw):
        parts.append(w_in[:, o:o + width])
        o += width
    wqa, wka, wva, wf, wqb, wkb, wvb = parts
    scale = 1.0 / math.sqrt(HEAD_DIM)
    w_all = jnp.concatenate(
        [_pad_heads(wqa * (scale * LOG2E)), _pad_heads(wka), _pad_heads(wva),
         wqb * scale, wkb, _pad_heads(wvb)], axis=1).astype(BF16)
    wf_pad = jnp.zeros((d, LANES), F32).at[:, :N_HEADS].set(wf)
    bf_pad = jnp.zeros((1, LANES), F32).at[0, :N_HEADS].set(b_forget)

    qa, ka, va, qb, kb, vb = _inproj(x2d, attn_norm_g.reshape(1, d), w_all, wf_pad, bf_pad, s)
    ya = _attention_call(_fox_kernel, "fox", qa, ka, va, (), s, 2 * LANES, 2 * LANES)
    tri = (jnp.arange(SUB)[:, None] >= jnp.arange(SUB)[None, :]).astype(BF16)
    yb = _attention_call(_sb_kernel, "sb", qb, kb, vb, (tri,), s, LANES, LANES)

    wr = jnp.zeros((d, LANES), F32).at[:, :N_EXPERTS].set(w_router)
    br = jnp.full((1, LANES), NEG_BIG, F32).at[0, :N_EXPERTS].set(b_router)
    x2, h2, meta, cnt = _outproj(x2d, ya, yb, fox_out_g.reshape(1, fw), sb_out_g.reshape(1, fw),
                                 w_out.astype(BF16), ffn_norm_g.reshape(1, d), wr, br)

    bm = BM_MOE
    n_blocks = (t * TOP_K + N_EXPERTS * (bm - 1)) // bm
    counts = cnt[0, :N_EXPERTS].astype(jnp.int32)
    padded = (counts + bm - 1) // bm * bm
    ends = jnp.cumsum(padded)
    starts = ends - padded
    top_idx = meta[:, 0:TOP_K].astype(jnp.int32)
    rank = meta[:, TOP_K:2 * TOP_K].astype(jnp.int32)
    dest = (starts[top_idx] + rank).reshape(t * TOP_K)
    block_start = jnp.arange(n_blocks, dtype=jnp.int32) * bm
    block_e = jnp.minimum(jnp.sum(block_start[:, None] >= ends[None, :], axis=1), N_EXPERTS - 1).astype(jnp.int32)
    first = jnp.concatenate([jnp.ones((1,), jnp.int32), (block_e[1:] != block_e[:-1]).astype(jnp.int32)])
    n_valid = (ends[-1:] // bm).astype(jnp.int32)

    xs = _dispatch(dest, h2, n_blocks * bm)
    dff = w_down.shape[1]
    ys = _moe(block_e, first, n_valid, xs, w_gate_up, w_down, b_gate_up.reshape(N_EXPERTS, 1, 2 * dff),
              b_down.reshape(N_EXPERTS, 1, d))
    out = _combine(dest, x2, meta, final_norm_g.reshape(1, d), ys)
    return out.reshape(b, s, d)
```

```python
import functools
import math

import jax
import jax.numpy as jnp
from jax import lax
from jax.experimental import pallas as pl
from jax.experimental.pallas import tpu as pltpu

F32 = jnp.float32
BF16 = jnp.bfloat16

HEAD_DIM = 64
N_HEADS = 8
N_EXPERTS = 32
TOP_K = 4
LANES = 128
NORM_EPS = 1e-5
SWIGLU_LIMIT = 7.0
SWIGLU_ALPHA = 1.702
LOG2E = math.log2(math.e)
NEG_BIG = -1e30
N_BIAS_PARTS = 3
EXP_UNDERFLOW = 110.0
EXP2_UNDERFLOW = 160.0

TM_PROJ = 512
BQ = 512
BK = 512
SUB = 256
TM_ROUTE = 256
TM_DISPATCH = 256
TM_COMBINE = 256
BM_MOE = 256
VMEM_LIMIT = 56 * 1024 * 1024


def _cparams(sem):
    return pltpu.CompilerParams(dimension_semantics=sem, vmem_limit_bytes=VMEM_LIMIT)


def _hdot(a, b):
    return jnp.dot(a, b, precision=lax.Precision.HIGHEST, preferred_element_type=F32)


def _bdot(a, b):
    return jnp.dot(a, b, preferred_element_type=F32)


def _nt_dot(a, b):
    return lax.dot_general(a, b, (((1,), (1,)), ((), ())), preferred_element_type=F32)


def _inproj_kernel(x_ref, g_ref, w_ref, wf_ref, bf_ref, tri_ref, place_ref,
                   qa_ref, ka_ref, va_ref, qb_ref, kb_ref, vb_ref, carry_ref, *, tiles_per_seq):
    i = pl.program_id(0)

    @pl.when(i % tiles_per_seq == 0)
    def _():
        carry_ref[...] = jnp.zeros_like(carry_ref)

    x = x_ref[...]
    ms = jnp.mean(x * x, axis=-1, keepdims=True)
    h = x * lax.rsqrt(ms + NORM_EPS) * g_ref[...]
    hb = h.astype(BF16)

    logit = _hdot(h, wf_ref[...]) + bf_ref[...]
    logf = jnp.minimum(logit, 0.0) - jnp.log(1.0 + jnp.exp(-jnp.abs(logit)))
    c = _hdot(tri_ref[...], logf) + carry_ref[0:1, :]
    tm = c.shape[0]
    carry_ref[...] = jnp.broadcast_to(c[tm - 1:tm, :], carry_ref.shape)
    rest = -LOG2E * c
    placed = jnp.zeros((tm, ka_ref.shape[1]), F32)
    for n in range(N_BIAS_PARTS):
        part = rest.astype(BF16)
        rest = rest - part.astype(F32)
        placed = placed + _bdot(part, place_ref[n])

    lane = lax.broadcasted_iota(jnp.int32, (1, qa_ref.shape[1]), 1) % LANES
    q_extra = jnp.where((lane >= HEAD_DIM) & (lane < HEAD_DIM + N_BIAS_PARTS), 1.0, 0.0)
    v_extra = jnp.where(lane >= HEAD_DIM, 1.0, 0.0)
    extras = (q_extra, placed, v_extra, None, None, None)
    offset = 0
    for ref, extra in zip((qa_ref, ka_ref, va_ref, qb_ref, kb_ref, vb_ref), extras):
        width = ref.shape[1]
        out = _bdot(hb, w_ref[:, offset:offset + width])
        ref[...] = (out if extra is None else out + extra).astype(BF16)
        offset += width


def _pad_heads(w):
    d = w.shape[0]
    w = w.reshape(d, N_HEADS, HEAD_DIM)
    return jnp.concatenate([w, jnp.zeros_like(w)], axis=2).reshape(d, N_HEADS * LANES)


def _inproj(x2d, g, w_all, wf, bf, seq):
    t, d = x2d.shape
    tm = TM_PROJ
    width = N_HEADS * LANES
    tiles_per_seq = seq // tm
    tri = (jnp.arange(tm)[:, None] >= jnp.arange(tm)[None, :]).astype(F32)
    src = jnp.arange(LANES)[:, None]
    dst = jnp.arange(width)[None, :]
    place = jnp.stack([((dst == src * LANES + HEAD_DIM + n) & (src < N_HEADS)).astype(BF16)
                       for n in range(N_BIAS_PARTS)])
    pair_width = N_HEADS * HEAD_DIM
    widths = (width, width, width, pair_width, pair_width, width)
    assert sum(widths) == w_all.shape[1]
    whole = lambda a: pl.BlockSpec(a.shape, lambda i: (0,) * a.ndim)
    return pl.pallas_call(
        functools.partial(_inproj_kernel, tiles_per_seq=tiles_per_seq),
        grid=(t // tm,),
        in_specs=[pl.BlockSpec((tm, d), lambda i: (i, 0)), whole(g), whole(w_all), whole(wf), whole(bf),
                  whole(tri), whole(place)],
        out_specs=[pl.BlockSpec((tm, w), lambda i: (i, 0)) for w in widths],
        out_shape=[jax.ShapeDtypeStruct((t, w), BF16) for w in widths],
        scratch_shapes=[pltpu.VMEM((8, LANES), F32)],
        compiler_params=_cparams(("arbitrary",)),
        name="inproj",
    )(x2d, g, w_all, wf, bf, tri, place)


def _pair_out(o_even, o_odd):
    lane = lax.broadcasted_iota(jnp.int32, o_even.shape, 1)
    return jnp.where(lane < HEAD_DIM, o_even, pltpu.roll(o_odd, HEAD_DIM, axis=1))


def _head_sq_norm(tile):
    tile = tile.astype(F32)
    lane = lax.broadcasted_iota(jnp.int32, tile.shape, 1)
    return jnp.sum(jnp.where(lane < HEAD_DIM, tile * tile, 0.0), axis=1, keepdims=True)


def _fox_kernel(q_ref, k_ref, v_ref, o_ref, kmax_ref):
    i = pl.program_id(2)
    bq = q_ref.shape[0]
    n_seq_blocks = k_ref.shape[0] // BK

    @pl.when(i == 0)
    def _():
        def body(n, best):
            start = pl.multiple_of(n * BK, BK)
            return tuple(
                jnp.maximum(best[hh], jnp.max(_head_sq_norm(k_ref[pl.ds(start, BK), hh * LANES:(hh + 1) * LANES]),
                                              axis=0, keepdims=True))
                for hh in range(2))
        best = lax.fori_loop(0, n_seq_blocks, body, (jnp.zeros((1, 1), F32), jnp.zeros((1, 1), F32)))
        for hh in range(2):
            kmax_ref[hh:hh + 1, :] = jnp.broadcast_to(jnp.sqrt(best[hh]), (1, LANES))

    q_heads = (q_ref[:, 0:LANES], q_ref[:, LANES:2 * LANES])
    reach = tuple(jnp.sqrt(_head_sq_norm(q_heads[hh])) * kmax_ref[hh:hh + 1, 0:1] for hh in range(2))
    row = lax.broadcasted_iota(jnp.int32, (bq, BK), 0)
    col = lax.broadcasted_iota(jnp.int32, (bq, BK), 1)
    causal = col <= row
    lane1 = lax.broadcasted_iota(jnp.int32, (1, LANES), 1)
    bias_lanes = (lane1 >= HEAD_DIM) & (lane1 < HEAD_DIM + N_BIAS_PARTS)

    def step(j, carry, diag):
        start = pl.multiple_of(j * BK, BK)
        new = []
        slack = None
        for hh in range(2):
            m, acc = carry[hh]
            ks = k_ref[pl.ds(start, BK), hh * LANES:(hh + 1) * LANES]
            vs = v_ref[pl.ds(start, BK), hh * LANES:(hh + 1) * LANES]
            s = _nt_dot(q_heads[hh], ks)
            if diag:
                s = jnp.where(causal, s, NEG_BIG)
            m_new = jnp.maximum(m, jnp.max(s, axis=1, keepdims=True))
            p = jnp.exp2(s - m_new)
            acc = jnp.exp2(m - m_new) * acc + _bdot(p.astype(BF16), vs)
            new.append((m_new, acc))
            first = k_ref[pl.ds(start, 1), hh * LANES:(hh + 1) * LANES].astype(F32)
            bias0 = jnp.sum(jnp.where(bias_lanes, first, 0.0), axis=1, keepdims=True)
            bound = jnp.max(reach[hh] + bias0 - m_new)
            slack = bound if slack is None else jnp.maximum(slack, bound)
        return tuple(new), (slack < -EXP2_UNDERFLOW).astype(jnp.int32)

    init = tuple((jnp.full((bq, 1), NEG_BIG, F32), jnp.zeros((bq, LANES), F32)) for _ in range(2))
    carry, done = step(i, init, True)

    def block(state):
        n, _, c = state
        c, done = step(i - 1 - n, c, False)
        return n + 1, done, c

    _, _, carry = lax.while_loop(lambda st: (st[0] < i) & (st[1] == 0), block, (jnp.int32(0), done, carry))
    (_, acc_a), (_, acc_b) = carry
    norm = lambda acc: acc / pltpu.roll(acc, HEAD_DIM, axis=1)
    o_ref[...] = _pair_out(norm(acc_a), norm(acc_b))


def _softplus(z):
    return jnp.maximum(z, 0.0) + jnp.log(1.0 + jnp.exp(-jnp.abs(z)))


def _sb_kernel(q_ref, k_ref, v_ref, tri_ref, o_ref):
    i = pl.program_id(2)
    bq = q_ref.shape[0]
    n_sub = bq // SUB
    q2 = q_ref[...]
    lane_q = lax.broadcasted_iota(jnp.int32, q2.shape, 1)
    q_heads = (jnp.where(lane_q < HEAD_DIM, q2, jnp.zeros_like(q2)),
               jnp.where(lane_q >= HEAD_DIM, q2, jnp.zeros_like(q2)))
    row = lax.broadcasted_iota(jnp.int32, (bq, SUB), 0)
    col = lax.broadcasted_iota(jnp.int32, (bq, SUB), 1)
    tri = tri_ref[...]

    def sub_step(start, carry, strict):
        ks = k_ref[pl.ds(start, SUB), :]
        new = []
        for hh in range(2):
            later, acc = carry[hh]
            vs = v_ref[pl.ds(start, SUB), hh * LANES:(hh + 1) * LANES]
            z = _nt_dot(q_heads[hh], ks)
            sp = _softplus(z)
            if strict is not None:
                sp = jnp.where(strict, sp, 0.0)
            hi = sp.astype(BF16)
            lo = (sp - hi.astype(F32)).astype(BF16)
            g = _bdot(hi, tri) + _bdot(lo, tri)
            a = jnp.exp(z - g - later)
            if strict is not None:
                a = jnp.where(strict, a, 0.0)
            acc = acc + _bdot(a.astype(BF16), vs)
            new.append((later + g[:, 0:1], acc))
        return tuple(new)

    carry = tuple((jnp.zeros((bq, 1), F32), jnp.zeros((bq, LANES), F32)) for _ in range(2))
    for u in reversed(range(n_sub)):
        carry = sub_step(pl.multiple_of(i * bq + u * SUB, SUB), carry, col + u * SUB < row)

    def decayed(c):
        return (jnp.min(jnp.minimum(c[0][0], c[1][0])) >= EXP_UNDERFLOW).astype(jnp.int32)

    def block(state):
        n, _, c = state
        base = (i - 1 - n) * bq
        for u in reversed(range(n_sub)):
            c = sub_step(pl.multiple_of(base + u * SUB, SUB), c, None)
        return n + 1, decayed(c), c

    _, _, carry = lax.while_loop(lambda st: (st[0] < i) & (st[1] == 0), block,
                                 (jnp.int32(0), decayed(carry), carry))
    o_ref[...] = _pair_out(carry[0][1], carry[1][1])


def _attention_call(body, name, q, k, v, extra, seq, q_lanes, k_lanes, scratch=()):
    t = q.shape[0]
    nb, nq, npair = t // seq, seq // BQ, N_HEADS // 2
    q_spec = lambda w: pl.BlockSpec((BQ, w), lambda b, p, i: (b * nq + i, p))
    kv_spec = lambda w: pl.BlockSpec((seq, w), lambda b, p, i: (b, p))
    return pl.pallas_call(
        body,
        grid=(nb, npair, nq),
        in_specs=[q_spec(q_lanes), kv_spec(k_lanes), kv_spec(2 * LANES)]
        + [pl.BlockSpec(a.shape, lambda b, p, i: (0,) * a.ndim) for a in extra],
        out_specs=q_spec(LANES),
        out_shape=jax.ShapeDtypeStruct((t, npair * LANES), F32),
        scratch_shapes=list(scratch),
        compiler_params=_cparams(("arbitrary", "arbitrary", "arbitrary")),
        name=name,
    )(q, k, v, *extra)


def _rms(y, g):
    return y * lax.rsqrt(jnp.mean(y * y, axis=-1, keepdims=True) + NORM_EPS) * g


def _outproj_kernel(x_ref, ya_ref, yb_ref, ga_ref, gb_ref, wo_ref, gf_ref, wr_ref, br_ref, ltri_ref,
                    x2_ref, h2_ref, meta_ref, cnt_ref, carry_ref):
    i = pl.program_id(0)

    @pl.when(i == 0)
    def _():
        carry_ref[...] = jnp.zeros_like(carry_ref)

    ya = _rms(ya_ref[...], ga_ref[...]).astype(BF16)
    yb = _rms(yb_ref[...], gb_ref[...]).astype(BF16)
    wa = ya_ref.shape[1]
    x2 = x_ref[...] + _bdot(ya, wo_ref[0:wa, :]) + _bdot(yb, wo_ref[wa:, :])
    x2_ref[...] = x2
    h2 = _rms(x2, gf_ref[...])
    h2_ref[...] = h2

    logits = _hdot(h2, wr_ref[...]) + br_ref[...]
    tm = logits.shape[0]
    lane = lax.broadcasted_iota(jnp.int32, (tm, LANES), 1)
    lane_f = lane.astype(F32)
    work = logits
    vals, sels, idxs = [], [], []
    for _ in range(TOP_K):
        mx = jnp.max(work, axis=1, keepdims=True)
        idx = jnp.min(jnp.where(work == mx, lane_f, float(LANES)), axis=1, keepdims=True)
        sel = lane_f == idx
        vals.append(mx)
        idxs.append(idx)
        sels.append(sel)
        work = jnp.where(sel, -jnp.inf, work)
    exps = [jnp.exp(v - vals[0]) for v in vals]
    denom = exps[0] + exps[1] + exps[2] + exps[3]
    gates = [e / denom for e in exps]

    onehot = jnp.zeros((tm, LANES), F32)
    for sel in sels:
        onehot = onehot + jnp.where(sel, 1.0, 0.0)
    before = _bdot(ltri_ref[...], onehot.astype(BF16)) + carry_ref[0:1, :]
    meta = jnp.zeros((tm, LANES), F32)
    for k in range(TOP_K):
        rank = jnp.sum(jnp.where(sels[k], before, 0.0), axis=1, keepdims=True)
        meta = jnp.where(lane == k, idxs[k], meta)
        meta = jnp.where(lane == TOP_K + k, rank, meta)
        meta = jnp.where(lane == 2 * TOP_K + k, gates[k], meta)
    meta_ref[...] = meta
    total = carry_ref[0:1, :] + jnp.sum(onehot, axis=0, keepdims=True)
    carry_ref[...] = jnp.broadcast_to(total, carry_ref.shape)
    cnt_ref[...] = jnp.broadcast_to(total, cnt_ref.shape)


def _outproj(x2d, ya, yb, ga, gb, wo, gf, wr, br):
    t, d = x2d.shape
    tm = TM_ROUTE
    wa = ya.shape[1]
    ltri = (jnp.arange(tm)[:, None] > jnp.arange(tm)[None, :]).astype(BF16)
    tile = lambda w: pl.BlockSpec((tm, w), lambda i: (i, 0))
    whole = lambda a: pl.BlockSpec(a.shape, lambda i: (0, 0))
    return pl.pallas_call(
        _outproj_kernel,
        grid=(t // tm,),
        in_specs=[tile(d), tile(wa), tile(wa), whole(ga), whole(gb), whole(wo), whole(gf),
                  whole(wr), whole(br), whole(ltri)],
        out_specs=[tile(d), tile(d), tile(LANES), pl.BlockSpec((8, LANES), lambda i: (0, 0))],
        out_shape=[jax.ShapeDtypeStruct((t, d), F32), jax.ShapeDtypeStruct((t, d), F32),
                   jax.ShapeDtypeStruct((t, LANES), F32), jax.ShapeDtypeStruct((8, LANES), F32)],
        scratch_shapes=[pltpu.VMEM((8, LANES), F32)],
        compiler_params=_cparams(("arbitrary",)),
        name="outproj",
    )(x2d, ya, yb, ga, gb, wo, gf, wr, br, ltri)


def _row_copy(src, src_row, dst, dst_row, sem):
    return pltpu.make_async_copy(src.at[pl.ds(src_row, 1), :], dst.at[pl.ds(dst_row, 1), :], sem)


def _dispatch_kernel(dest_ref, h_ref, xs_init_ref, xs_ref, sem):
    del xs_init_ref
    tm = h_ref.shape[0]

    def start(t, _):
        for k in range(TOP_K):
            _row_copy(h_ref, t, xs_ref, dest_ref[t * TOP_K + k], sem).start(priority=k % 2)
        return 0

    def wait(t, _):
        for k in range(TOP_K):
            _row_copy(h_ref, t, xs_ref, dest_ref[t * TOP_K + k], sem).wait()
        return 0

    lax.fori_loop(0, tm, start, 0)
    lax.fori_loop(0, tm, wait, 0)


def _dispatch(dest_flat, h2, n_rows):
    t, d = h2.shape
    tm = TM_DISPATCH
    xs_init = jnp.zeros((n_rows, d), h2.dtype)
    return pl.pallas_call(
        _dispatch_kernel,
        grid=(t // tm,),
        in_specs=[
            pl.BlockSpec((tm * TOP_K,), lambda i: (i,), memory_space=pltpu.SMEM),
            pl.BlockSpec((tm, d), lambda i: (i, 0)),
            pl.BlockSpec(memory_space=pl.ANY),
        ],
        out_specs=pl.BlockSpec(memory_space=pl.ANY),
        out_shape=jax.ShapeDtypeStruct((n_rows, d), h2.dtype),
        scratch_shapes=[pltpu.SemaphoreType.DMA],
        input_output_aliases={2: 0},
        compiler_params=_cparams(("arbitrary",)),
        name="dispatch",
    )(dest_flat, h2, xs_init)


def _moe_kernel(be_ref, first_ref, nv_ref, xs_ref, wgu_ref, wd_ref, bgu_ref, bd_ref, ys_ref,
                wgu_bf_ref, wd2_ref):
    del be_ref
    i = pl.program_id(0)

    @pl.when(first_ref[i] == 1)
    def _():
        wgu_bf_ref[...] = wgu_ref[0].astype(BF16)
        wd = wd_ref[0].astype(BF16).astype(F32)
        bits = pltpu.bitcast(wd, jnp.uint32)
        wd2_ref[...] = bits | (bits >> 16)

    @pl.when(i < nv_ref[0])
    def _():
        xb = xs_ref[...].astype(BF16)
        gu = _bdot(xb, wgu_bf_ref[...]) + bgu_ref[0]
        even = lax.broadcasted_iota(jnp.int32, (gu.shape[0], LANES), 1) % 2 == 0
        acts = []
        for c in range(gu.shape[1] // LANES):
            blk = gu[:, c * LANES:(c + 1) * LANES]
            gate = jnp.minimum(blk, SWIGLU_LIMIT)
            glu = gate * (1.0 / (1.0 + jnp.exp(-SWIGLU_ALPHA * gate)))
            up1 = jnp.clip(blk, -SWIGLU_LIMIT, SWIGLU_LIMIT) + 1.0
            acts.append(jnp.where(even, glu * pltpu.roll(up1, LANES - 1, axis=1), 0.0).astype(BF16))
        act = jnp.concatenate(acts, axis=1)
        wd2 = pltpu.bitcast(wd2_ref[...], BF16)
        ys_ref[...] = _bdot(act, wd2) + bd_ref[0]

    @pl.when(i >= nv_ref[0])
    def _():
        ys_ref[...] = jnp.zeros_like(ys_ref)


def _moe(block_e, first, n_valid, xs, wgu, wd, bgu, bd):
    n_rows, d = xs.shape
    bm = BM_MOE
    dff = wd.shape[1]
    w_spec = lambda a: pl.BlockSpec((1,) + a.shape[1:], lambda i, be, fi, nv: (be[i], 0, 0))
    grid_spec = pltpu.PrefetchScalarGridSpec(
        num_scalar_prefetch=3,
        grid=(n_rows // bm,),
        in_specs=[pl.BlockSpec((bm, d), lambda i, be, fi, nv: (i, 0)),
                  w_spec(wgu), w_spec(wd), w_spec(bgu), w_spec(bd)],
        out_specs=pl.BlockSpec((bm, d), lambda i, be, fi, nv: (i, 0)),
        scratch_shapes=[pltpu.VMEM((d, 2 * dff), BF16), pltpu.VMEM((dff, d), jnp.uint32)],
    )
    return pl.pallas_call(
        _moe_kernel,
        grid_spec=grid_spec,
        out_shape=jax.ShapeDtypeStruct((n_rows, d), F32),
        compiler_params=_cparams(("arbitrary",)),
        name="moe",
    )(block_e, first, n_valid, xs, wgu, wd, bgu, bd)


def _combine_kernel(dest_ref, x2_ref, meta_ref, g_ref, ys_ref, o_ref, buf_ref, sem):
    tm = x2_ref.shape[0]

    def start(t, _):
        for k in range(TOP_K):
            _row_copy(ys_ref, dest_ref[t * TOP_K + k], buf_ref.at[k], t, sem).start(priority=k % 2)
        return 0

    def wait(t, _):
        for k in range(TOP_K):
            _row_copy(ys_ref, dest_ref[t * TOP_K + k], buf_ref.at[k], t, sem).wait()
        return 0

    lax.fori_loop(0, tm, start, 0)
    lax.fori_loop(0, tm, wait, 0)
    meta = meta_ref[...]
    y = x2_ref[...]
    for k in range(TOP_K):
        y = y + meta[:, 2 * TOP_K + k:2 * TOP_K + k + 1] * buf_ref[k]
    o_ref[...] = _rms(y, g_ref[...])


def _combine(dest_flat, x2, meta, g, ys):
    t, d = x2.shape
    tm = TM_COMBINE
    return pl.pallas_call(
        _combine_kernel,
        grid=(t // tm,),
        in_specs=[
            pl.BlockSpec((tm * TOP_K,), lambda i: (i,), memory_space=pltpu.SMEM),
            pl.BlockSpec((tm, d), lambda i: (i, 0)),
            pl.BlockSpec((tm, LANES), lambda i: (i, 0)),
            pl.BlockSpec((1, d), lambda i: (0, 0)),
            pl.BlockSpec(memory_space=pl.ANY),
        ],
        out_specs=pl.BlockSpec((tm, d), lambda i: (i, 0)),
        out_shape=jax.ShapeDtypeStruct((t, d), F32),
        scratch_shapes=[pltpu.VMEM((TOP_K, tm, d), F32), pltpu.SemaphoreType.DMA],
        compiler_params=_cparams(("arbitrary",)),
        name="combine",
    )(dest_flat, x2, meta, g, ys)


def kernel(x, attn_norm_g, w_in, b_forget, fox_out_g, sb_out_g, w_out, ffn_norm_g, w_router, b_router,
           w_gate_up, b_gate_up, w_down, b_down, final_norm_g):
    b, s, d = x.shape
    t = b * s
    fw = N_HEADS * HEAD_DIM
    x2d = x.reshape(t, d)

    o = 0
    parts = []
    for width in (fw, fw, fw, N_HEADS, fw, fw, fw):
        parts.append(w_in[:, o:o + width])
        o += width
    wqa, wka, wva, wf, wqb, wkb, wvb = parts
    scale = 1.0 / math.sqrt(HEAD_DIM)
    w_all = jnp.concatenate(
        [_pad_heads(wqa * (scale * LOG2E)), _pad_heads(wka), _pad_heads(wva),
         wqb * scale, wkb, _pad_heads(wvb)], axis=1).astype(BF16)
    wf_pad = jnp.zeros((d, LANES), F32).at[:, :N_HEADS].set(wf)
    bf_pad = jnp.zeros((1, LANES), F32).at[0, :N_HEADS].set(b_forget)

    qa, ka, va, qb, kb, vb = _inproj(x2d, attn_norm_g.reshape(1, d), w_all, wf_pad, bf_pad, s)
    ya = _attention_call(_fox_kernel, "fox", qa, ka, va, (), s, 2 * LANES, 2 * LANES,
                         scratch=(pltpu.VMEM((8, LANES), F32),))
    tri = (jnp.arange(SUB)[:, None] >= jnp.arange(SUB)[None, :]).astype(BF16)
    yb = _attention_call(_sb_kernel, "sb", qb, kb, vb, (tri,), s, LANES, LANES)

    wr = jnp.zeros((d, LANES), F32).at[:, :N_EXPERTS].set(w_router)
    br = jnp.full((1, LANES), NEG_BIG, F32).at[0, :N_EXPERTS].set(b_router)
    x2, h2, meta, cnt = _outproj(x2d, ya, yb, fox_out_g.reshape(1, fw), sb_out_g.reshape(1, fw),
                                 w_out.astype(BF16), ffn_norm_g.reshape(1, d), wr, br)

    bm = BM_MOE
    n_blocks = (t * TOP_K + N_EXPERTS * (bm - 1)) // bm
    counts = cnt[0, :N_EXPERTS].astype(jnp.int32)
    padded = (counts + bm - 1) // bm * bm
    ends = jnp.cumsum(padded)
    starts = ends - padded
    top_idx = meta[:, 0:TOP_K].astype(jnp.int32)
    rank = meta[:, TOP_K:2 * TOP_K].astype(jnp.int32)
    dest = (starts[top_idx] + rank).reshape(t * TOP_K)
    block_start = jnp.arange(n_blocks, dtype=jnp.int32) * bm
    block_e = jnp.minimum(jnp.sum(block_start[:, None] >= ends[None, :], axis=1), N_EXPERTS - 1).astype(jnp.int32)
    first = jnp.concatenate([jnp.ones((1,), jnp.int32), (block_e[1:] != block_e[:-1]).astype(jnp.int32)])
    n_valid = (ends[-1:] // bm).astype(jnp.int32)

    xs = _dispatch(dest, h2, n_blocks * bm)
    dff = w_down.shape[1]
    ys = _moe(block_e, first, n_valid, xs, w_gate_up, w_down, b_gate_up.reshape(N_EXPERTS, 1, 2 * dff),
              b_down.reshape(N_EXPERTS, 1, d))
    out = _combine(dest, x2, meta, final_norm_g.reshape(1, d), ys)
    return out.reshape(b, s, d)
```

```python
import functools
import math

import jax
import jax.numpy as jnp
from jax import lax
from jax.experimental import pallas as pl
from jax.experimental.pallas import tpu as pltpu

F32 = jnp.float32
BF16 = jnp.bfloat16

HEAD_DIM = 64
N_HEADS = 8
N_EXPERTS = 32
TOP_K = 4
LANES = 128
NORM_EPS = 1e-5
SWIGLU_LIMIT = 7.0
SWIGLU_ALPHA = 1.702
LOG2E = math.log2(math.e)
NEG_BIG = -1e30
N_BIAS_PARTS = 3
EXP_UNDERFLOW = 110.0
EXP2_UNDERFLOW = 160.0

TM_PROJ = 512
BQ = 512
BK = 512
SUB = 256
TM_ROUTE = 512
TM_DISPATCH = 256
TM_COMBINE = 256
BM_MOE = 256
VMEM_LIMIT = 56 * 1024 * 1024


def _cparams(sem):
    return pltpu.CompilerParams(dimension_semantics=sem, vmem_limit_bytes=VMEM_LIMIT)


def _bdot(a, b):
    return jnp.dot(a, b, preferred_element_type=F32)


def _split_bf16(x, parts):
    out = []
    for _ in range(parts):
        piece = x.astype(BF16)
        out.append(piece)
        x = x - piece.astype(F32)
    return out


def _dot3(a_hi, a_lo, b_hi_ref, b_lo_ref):
    return _bdot(a_hi, b_hi_ref[...]) + _bdot(a_lo, b_hi_ref[...]) + _bdot(a_hi, b_lo_ref[...])


def _nt_dot(a, b):
    return lax.dot_general(a, b, (((1,), (1,)), ((), ())), preferred_element_type=F32)


def _inproj_kernel(x_ref, g_ref, w_ref, wf_hi_ref, wf_lo_ref, bf_ref, tri_ref, place_ref,
                   qa_ref, ka_ref, va_ref, qb_ref, kb_ref, vb_ref, carry_ref, *, tiles_per_seq):
    i = pl.program_id(0)

    @pl.when(i % tiles_per_seq == 0)
    def _():
        carry_ref[...] = jnp.zeros_like(carry_ref)

    x = x_ref[...]
    ms = jnp.mean(x * x, axis=-1, keepdims=True)
    h = x * lax.rsqrt(ms + NORM_EPS) * g_ref[...]
    hb, h_lo = _split_bf16(h, 2)

    logit = _dot3(hb, h_lo, wf_hi_ref, wf_lo_ref) + bf_ref[...]
    logf = jnp.minimum(logit, 0.0) - jnp.log(1.0 + jnp.exp(-jnp.abs(logit)))
    c = carry_ref[0:1, :]
    for piece in _split_bf16(logf, 3):
        c = c + _bdot(tri_ref[...], piece)
    tm = c.shape[0]
    carry_ref[...] = jnp.broadcast_to(c[tm - 1:tm, :], carry_ref.shape)
    placed = jnp.zeros((tm, ka_ref.shape[1]), F32)
    for n, part in enumerate(_split_bf16(-LOG2E * c, N_BIAS_PARTS)):
        placed = placed + _bdot(part, place_ref[n])

    lane = lax.broadcasted_iota(jnp.int32, (1, qa_ref.shape[1]), 1) % LANES
    q_extra = jnp.where((lane >= HEAD_DIM) & (lane < HEAD_DIM + N_BIAS_PARTS), 1.0, 0.0)
    v_extra = jnp.where(lane >= HEAD_DIM, 1.0, 0.0)
    extras = (q_extra, placed, v_extra, None, None, None)
    offset = 0
    for ref, extra in zip((qa_ref, ka_ref, va_ref, qb_ref, kb_ref, vb_ref), extras):
        width = ref.shape[1]
        out = _bdot(hb, w_ref[:, offset:offset + width])
        ref[...] = (out if extra is None else out + extra).astype(BF16)
        offset += width


def _pad_heads(w):
    d = w.shape[0]
    w = w.reshape(d, N_HEADS, HEAD_DIM)
    return jnp.concatenate([w, jnp.zeros_like(w)], axis=2).reshape(d, N_HEADS * LANES)


def _inproj(x2d, g, w_all, wf, bf, seq):
    t, d = x2d.shape
    tm = TM_PROJ
    width = N_HEADS * LANES
    tiles_per_seq = seq // tm
    tri = (jnp.arange(tm)[:, None] >= jnp.arange(tm)[None, :]).astype(BF16)
    wf_hi = wf.astype(BF16)
    wf_lo = (wf - wf_hi.astype(F32)).astype(BF16)
    src = jnp.arange(LANES)[:, None]
    dst = jnp.arange(width)[None, :]
    place = jnp.stack([((dst == src * LANES + HEAD_DIM + n) & (src < N_HEADS)).astype(BF16)
                       for n in range(N_BIAS_PARTS)])
    pair_width = N_HEADS * HEAD_DIM
    widths = (width, width, width, pair_width, pair_width, width)
    assert sum(widths) == w_all.shape[1]
    whole = lambda a: pl.BlockSpec(a.shape, lambda i: (0,) * a.ndim)
    return pl.pallas_call(
        functools.partial(_inproj_kernel, tiles_per_seq=tiles_per_seq),
        grid=(t // tm,),
        in_specs=[pl.BlockSpec((tm, d), lambda i: (i, 0)), whole(g), whole(w_all), whole(wf_hi), whole(wf_lo),
                  whole(bf), whole(tri), whole(place)],
        out_specs=[pl.BlockSpec((tm, w), lambda i: (i, 0)) for w in widths],
        out_shape=[jax.ShapeDtypeStruct((t, w), BF16) for w in widths],
        scratch_shapes=[pltpu.VMEM((8, LANES), F32)],
        compiler_params=_cparams(("arbitrary",)),
        name="inproj",
    )(x2d, g, w_all, wf_hi, wf_lo, bf, tri, place)


def _pair_out(o_even, o_odd):
    lane = lax.broadcasted_iota(jnp.int32, o_even.shape, 1)
    return jnp.where(lane < HEAD_DIM, o_even, pltpu.roll(o_odd, HEAD_DIM, axis=1))


def _head_sq_norm(tile):
    tile = tile.astype(F32)
    lane = lax.broadcasted_iota(jnp.int32, tile.shape, 1)
    return jnp.sum(jnp.where(lane < HEAD_DIM, tile * tile, 0.0), axis=1, keepdims=True)


def _fox_kernel(q_ref, k_ref, v_ref, o_ref, kmax_ref):
    i = pl.program_id(2)
    bq = q_ref.shape[0]
    n_seq_blocks = k_ref.shape[0] // BK

    @pl.when(i == 0)
    def _():
        def body(n, best):
            start = pl.multiple_of(n * BK, BK)
            return tuple(
                jnp.maximum(best[hh], jnp.max(_head_sq_norm(k_ref[pl.ds(start, BK), hh * LANES:(hh + 1) * LANES]),
                                              axis=0, keepdims=True))
                for hh in range(2))
        best = lax.fori_loop(0, n_seq_blocks, body, (jnp.zeros((1, 1), F32), jnp.zeros((1, 1), F32)))
        for hh in range(2):
            kmax_ref[hh:hh + 1, :] = jnp.broadcast_to(jnp.sqrt(best[hh]), (1, LANES))

    q_heads = (q_ref[:, 0:LANES], q_ref[:, LANES:2 * LANES])
    reach = tuple(jnp.sqrt(_head_sq_norm(q_heads[hh])) * kmax_ref[hh:hh + 1, 0:1] for hh in range(2))
    row = lax.broadcasted_iota(jnp.int32, (bq, BK), 0)
    col = lax.broadcasted_iota(jnp.int32, (bq, BK), 1)
    causal = col <= row
    lane1 = lax.broadcasted_iota(jnp.int32, (1, LANES), 1)
    bias_lanes = (lane1 >= HEAD_DIM) & (lane1 < HEAD_DIM + N_BIAS_PARTS)

    def step(j, carry, diag):
        start = pl.multiple_of(j * BK, BK)
        new = []
        slack = None
        for hh in range(2):
            m, acc = carry[hh]
            ks = k_ref[pl.ds(start, BK), hh * LANES:(hh + 1) * LANES]
            vs = v_ref[pl.ds(start, BK), hh * LANES:(hh + 1) * LANES]
            s = _nt_dot(q_heads[hh], ks)
            if diag:
                s = jnp.where(causal, s, NEG_BIG)
            m_new = jnp.maximum(m, jnp.max(s, axis=1, keepdims=True))
            p = jnp.exp2(s - m_new)
            acc = jnp.exp2(m - m_new) * acc + _bdot(p.astype(BF16), vs)
            new.append((m_new, acc))
            first = k_ref[pl.ds(start, 1), hh * LANES:(hh + 1) * LANES].astype(F32)
            bias0 = jnp.sum(jnp.where(bias_lanes, first, 0.0), axis=1, keepdims=True)
            bound = jnp.max(reach[hh] + bias0 - m_new)
            slack = bound if slack is None else jnp.maximum(slack, bound)
        return tuple(new), (slack < -EXP2_UNDERFLOW).astype(jnp.int32)

    init = tuple((jnp.full((bq, 1), NEG_BIG, F32), jnp.zeros((bq, LANES), F32)) for _ in range(2))
    carry, done = step(i, init, True)

    def block(state):
        n, _, c = state
        c, done = step(i - 1 - n, c, False)
        return n + 1, done, c

    _, _, carry = lax.while_loop(lambda st: (st[0] < i) & (st[1] == 0), block, (jnp.int32(0), done, carry))
    (_, acc_a), (_, acc_b) = carry
    norm = lambda acc: acc / pltpu.roll(acc, HEAD_DIM, axis=1)
    o_ref[...] = _pair_out(norm(acc_a), norm(acc_b))


def _softplus(z):
    return jnp.maximum(z, 0.0) + jnp.log(1.0 + jnp.exp(-jnp.abs(z)))


def _sb_kernel(q_ref, k_ref, v_ref, tri_ref, o_ref):
    i = pl.program_id(2)
    bq = q_ref.shape[0]
    n_sub = bq // SUB
    q2 = q_ref[...]
    lane_q = lax.broadcasted_iota(jnp.int32, q2.shape, 1)
    q_heads = (jnp.where(lane_q < HEAD_DIM, q2, jnp.zeros_like(q2)),
               jnp.where(lane_q >= HEAD_DIM, q2, jnp.zeros_like(q2)))
    row = lax.broadcasted_iota(jnp.int32, (SUB, SUB), 0)
    col = lax.broadcasted_iota(jnp.int32, (SUB, SUB), 1)
    strict = col < row
    tri = tri_ref[...]

    def sub_step(start, carry, first_row):
        r0 = 0 if first_row is None else first_row

        def mask_top(x):
            top = jnp.where(strict, x[:SUB], 0.0)
            return top if x.shape[0] == SUB else jnp.concatenate([top, x[SUB:]], axis=0)

        def add_rows(full, part):
            return full + part if r0 == 0 else jnp.concatenate([full[:r0], full[r0:] + part], axis=0)

        ks = k_ref[pl.ds(start, SUB), :]
        new = []
        for hh in range(2):
            later, acc = carry[hh]
            vs = v_ref[pl.ds(start, SUB), hh * LANES:(hh + 1) * LANES]
            z = _nt_dot(q_heads[hh][r0:], ks)
            sp = _softplus(z)
            if first_row is not None:
                sp = mask_top(sp)
            hi = sp.astype(BF16)
            lo = (sp - hi.astype(F32)).astype(BF16)
            g = _bdot(hi, tri) + _bdot(lo, tri)
            a = jnp.exp(z - g - later[r0:])
            if first_row is not None:
                a = mask_top(a)
            new.append((add_rows(later, g[:, 0:1]), add_rows(acc, _bdot(a.astype(BF16), vs))))
        return tuple(new)

    carry = tuple((jnp.zeros((bq, 1), F32), jnp.zeros((bq, LANES), F32)) for _ in range(2))
    for u in reversed(range(n_sub)):
        carry = sub_step(pl.multiple_of(i * bq + u * SUB, SUB), carry, u * SUB)

    def decayed(c):
        return (jnp.min(jnp.minimum(c[0][0], c[1][0])) >= EXP_UNDERFLOW).astype(jnp.int32)

    def back(state):
        n, _, c = state
        c = sub_step(pl.multiple_of(i * bq - (n + 1) * SUB, SUB), c, None)
        return n + 1, decayed(c), c

    _, _, carry = lax.while_loop(lambda st: (st[0] < i * n_sub) & (st[1] == 0), back,
                                 (jnp.int32(0), decayed(carry), carry))
    o_ref[...] = _pair_out(carry[0][1], carry[1][1])


def _attention_call(body, name, q, k, v, extra, seq, q_lanes, k_lanes, scratch=()):
    t = q.shape[0]
    nb, nq, npair = t // seq, seq // BQ, N_HEADS // 2
    q_spec = lambda w: pl.BlockSpec((BQ, w), lambda b, p, i: (b * nq + i, p))
    kv_spec = lambda w: pl.BlockSpec((seq, w), lambda b, p, i: (b, p))
    return pl.pallas_call(
        body,
        grid=(nb, npair, nq),
        in_specs=[q_spec(q_lanes), kv_spec(k_lanes), kv_spec(2 * LANES)]
        + [pl.BlockSpec(a.shape, lambda b, p, i: (0,) * a.ndim) for a in extra],
        out_specs=q_spec(LANES),
        out_shape=jax.ShapeDtypeStruct((t, npair * LANES), F32),
        scratch_shapes=list(scratch),
        compiler_params=_cparams(("arbitrary", "arbitrary", "arbitrary")),
        name=name,
    )(q, k, v, *extra)


def _rms(y, g):
    return y * lax.rsqrt(jnp.mean(y * y, axis=-1, keepdims=True) + NORM_EPS) * g


def _pack_bf16_pairs(x):
    n = x.shape[1] // 2
    lo = pltpu.bitcast(x[:, :n].astype(BF16).astype(F32), jnp.uint32) >> 16
    hi = pltpu.bitcast(x[:, n:].astype(BF16).astype(F32), jnp.uint32)
    return hi | lo


def _unpack_bf16_pairs(p):
    lo = pltpu.bitcast(p << 16, F32)
    hi = pltpu.bitcast(p & jnp.uint32(0xFFFF0000), F32)
    return jnp.concatenate([lo, hi], axis=1)


def _outproj_kernel(x_ref, ya_ref, yb_ref, ga_ref, gb_ref, wo_ref, gf_ref, wr_hi_ref, wr_lo_ref, br_ref,
                    ltri_ref, x2_ref, h2p_ref, meta_ref, cnt_ref, carry_ref):
    i = pl.program_id(0)

    @pl.when(i == 0)
    def _():
        carry_ref[...] = jnp.zeros_like(carry_ref)

    ya = _rms(ya_ref[...], ga_ref[...]).astype(BF16)
    yb = _rms(yb_ref[...], gb_ref[...]).astype(BF16)
    wa = ya_ref.shape[1]
    x2 = x_ref[...] + _bdot(ya, wo_ref[0:wa, :]) + _bdot(yb, wo_ref[wa:, :])
    x2_ref[...] = x2
    h2 = _rms(x2, gf_ref[...])
    h2p_ref[...] = _pack_bf16_pairs(h2)

    h2_hi, h2_lo = _split_bf16(h2, 2)
    logits = _dot3(h2_hi, h2_lo, wr_hi_ref, wr_lo_ref) + br_ref[...]
    tm = logits.shape[0]
    lane = lax.broadcasted_iota(jnp.int32, (tm, LANES), 1)
    lane_f = lane.astype(F32)
    work = logits
    vals, sels, idxs = [], [], []
    for _ in range(TOP_K):
        mx = jnp.max(work, axis=1, keepdims=True)
        idx = jnp.min(jnp.where(work == mx, lane_f, float(LANES)), axis=1, keepdims=True)
        sel = lane_f == idx
        vals.append(mx)
        idxs.append(idx)
        sels.append(sel)
        work = jnp.where(sel, -jnp.inf, work)
    exps = [jnp.exp(v - vals[0]) for v in vals]
    denom = exps[0] + exps[1] + exps[2] + exps[3]
    gates = [e / denom for e in exps]

    onehot = jnp.zeros((tm, LANES), F32)
    for sel in sels:
        onehot = onehot + jnp.where(sel, 1.0, 0.0)
    before = _bdot(ltri_ref[...], onehot.astype(BF16)) + carry_ref[0:1, :]
    meta = jnp.zeros((tm, LANES), F32)
    for k in range(TOP_K):
        rank = jnp.sum(jnp.where(sels[k], before, 0.0), axis=1, keepdims=True)
        meta = jnp.where(lane == k, idxs[k], meta)
        meta = jnp.where(lane == TOP_K + k, rank, meta)
        meta = jnp.where(lane == 2 * TOP_K + k, gates[k], meta)
    meta_ref[...] = meta
    total = carry_ref[0:1, :] + jnp.sum(onehot, axis=0, keepdims=True)
    carry_ref[...] = jnp.broadcast_to(total, carry_ref.shape)
    cnt_ref[...] = jnp.broadcast_to(total, cnt_ref.shape)


def _outproj(x2d, ya, yb, ga, gb, wo, gf, wr, br):
    t, d = x2d.shape
    tm = TM_ROUTE
    wa = ya.shape[1]
    ltri = (jnp.arange(tm)[:, None] > jnp.arange(tm)[None, :]).astype(BF16)
    wr_hi = wr.astype(BF16)
    wr_lo = (wr - wr_hi.astype(F32)).astype(BF16)
    tile = lambda w: pl.BlockSpec((tm, w), lambda i: (i, 0))
    whole = lambda a: pl.BlockSpec(a.shape, lambda i: (0, 0))
    return pl.pallas_call(
        _outproj_kernel,
        grid=(t // tm,),
        in_specs=[tile(d), tile(wa), tile(wa), whole(ga), whole(gb), whole(wo), whole(gf),
                  whole(wr_hi), whole(wr_lo), whole(br), whole(ltri)],
        out_specs=[tile(d), tile(d // 2), tile(LANES), pl.BlockSpec((8, LANES), lambda i: (0, 0))],
        out_shape=[jax.ShapeDtypeStruct((t, d), F32), jax.ShapeDtypeStruct((t, d // 2), jnp.uint32),
                   jax.ShapeDtypeStruct((t, LANES), F32), jax.ShapeDtypeStruct((8, LANES), F32)],
        scratch_shapes=[pltpu.VMEM((8, LANES), F32)],
        compiler_params=_cparams(("arbitrary",)),
        name="outproj",
    )(x2d, ya, yb, ga, gb, wo, gf, wr_hi, wr_lo, br, ltri)


def _row_copy(src, src_row, dst, dst_row, sem):
    return pltpu.make_async_copy(src.at[pl.ds(src_row, 1), :], dst.at[pl.ds(dst_row, 1), :], sem)


def _dispatch_kernel(dest_ref, h_ref, xs_init_ref, xs_ref, sem):
    del xs_init_ref
    tm = h_ref.shape[0]

    def start(t, _):
        for k in range(TOP_K):
            _row_copy(h_ref, t, xs_ref, dest_ref[t * TOP_K + k], sem).start(priority=k % 2)
        return 0

    def wait(t, _):
        for k in range(TOP_K):
            _row_copy(h_ref, t, xs_ref, dest_ref[t * TOP_K + k], sem).wait()
        return 0

    lax.fori_loop(0, tm, start, 0)
    lax.fori_loop(0, tm, wait, 0)


def _dispatch(dest_flat, h2, n_rows):
    t, d = h2.shape
    tm = TM_DISPATCH
    xs_init = jnp.zeros((n_rows, d), h2.dtype)
    return pl.pallas_call(
        _dispatch_kernel,
        grid=(t // tm,),
        in_specs=[
            pl.BlockSpec((tm * TOP_K,), lambda i: (i,), memory_space=pltpu.SMEM),
            pl.BlockSpec((tm, d), lambda i: (i, 0)),
            pl.BlockSpec(memory_space=pl.ANY),
        ],
        out_specs=pl.BlockSpec(memory_space=pl.ANY),
        out_shape=jax.ShapeDtypeStruct((n_rows, d), h2.dtype),
        scratch_shapes=[pltpu.SemaphoreType.DMA],
        input_output_aliases={2: 0},
        compiler_params=_cparams(("arbitrary",)),
        name="dispatch",
    )(dest_flat, h2, xs_init)


def _moe_kernel(be_ref, first_ref, nv_ref, xs_ref, wgu_ref, wd_ref, bgu_ref, bd_ref, ys_ref,
                wgu_bf_ref, wd2_ref):
    del be_ref
    i = pl.program_id(0)

    @pl.when(first_ref[i] == 1)
    def _():
        wgu_bf_ref[...] = wgu_ref[0].astype(BF16)
        wd = wd_ref[0].astype(BF16).astype(F32)
        bits = pltpu.bitcast(wd, jnp.uint32)
        wd2_ref[...] = bits | (bits >> 16)

    @pl.when(i < nv_ref[0])
    def _():
        xb = _unpack_bf16_pairs(xs_ref[...]).astype(BF16)
        gu = _bdot(xb, wgu_bf_ref[...]) + bgu_ref[0]
        even = lax.broadcasted_iota(jnp.int32, (gu.shape[0], LANES), 1) % 2 == 0
        acts = []
        for c in range(gu.shape[1] // LANES):
            blk = gu[:, c * LANES:(c + 1) * LANES]
            gate = jnp.minimum(blk, SWIGLU_LIMIT)
            glu = gate * (1.0 / (1.0 + jnp.exp(-SWIGLU_ALPHA * gate)))
            up1 = jnp.clip(blk, -SWIGLU_LIMIT, SWIGLU_LIMIT) + 1.0
            acts.append(jnp.where(even, glu * pltpu.roll(up1, LANES - 1, axis=1), 0.0).astype(BF16))
        act = jnp.concatenate(acts, axis=1)
        wd2 = pltpu.bitcast(wd2_ref[...], BF16)
        ys_ref[...] = _pack_bf16_pairs(_bdot(act, wd2) + bd_ref[0])

    @pl.when(i >= nv_ref[0])
    def _():
        ys_ref[...] = jnp.zeros_like(ys_ref)


def _moe(block_e, first, n_valid, xs, wgu, wd, bgu, bd):
    n_rows, half = xs.shape
    bm = BM_MOE
    dff, d = wd.shape[1:]
    w_spec = lambda a: pl.BlockSpec((1,) + a.shape[1:], lambda i, be, fi, nv: (be[i], 0, 0))
    grid_spec = pltpu.PrefetchScalarGridSpec(
        num_scalar_prefetch=3,
        grid=(n_rows // bm,),
        in_specs=[pl.BlockSpec((bm, half), lambda i, be, fi, nv: (i, 0)),
                  w_spec(wgu), w_spec(wd), w_spec(bgu), w_spec(bd)],
        out_specs=pl.BlockSpec((bm, d // 2), lambda i, be, fi, nv: (i, 0)),
        scratch_shapes=[pltpu.VMEM((d, 2 * dff), BF16), pltpu.VMEM((dff, d), jnp.uint32)],
    )
    return pl.pallas_call(
        _moe_kernel,
        grid_spec=grid_spec,
        out_shape=jax.ShapeDtypeStruct((n_rows, d // 2), jnp.uint32),
        compiler_params=_cparams(("arbitrary",)),
        name="moe",
    )(block_e, first, n_valid, xs, wgu, wd, bgu, bd)


def _combine_kernel(dest_ref, x2_ref, meta_ref, g_ref, ys_ref, o_ref, buf_ref, sem):
    tm = x2_ref.shape[0]

    def start(t, _):
        for k in range(TOP_K):
            _row_copy(ys_ref, dest_ref[t * TOP_K + k], buf_ref.at[k], t, sem).start(priority=k % 2)
        return 0

    def wait(t, _):
        for k in range(TOP_K):
            _row_copy(ys_ref, dest_ref[t * TOP_K + k], buf_ref.at[k], t, sem).wait()
        return 0

    lax.fori_loop(0, tm, start, 0)
    lax.fori_loop(0, tm, wait, 0)
    meta = meta_ref[...]
    y = x2_ref[...]
    for k in range(TOP_K):
        y = y + meta[:, 2 * TOP_K + k:2 * TOP_K + k + 1] * _unpack_bf16_pairs(buf_ref[k])
    o_ref[...] = _rms(y, g_ref[...])


def _combine(dest_flat, x2, meta, g, ys):
    t, d = x2.shape
    tm = TM_COMBINE
    return pl.pallas_call(
        _combine_kernel,
        grid=(t // tm,),
        in_specs=[
            pl.BlockSpec((tm * TOP_K,), lambda i: (i,), memory_space=pltpu.SMEM),
            pl.BlockSpec((tm, d), lambda i: (i, 0)),
            pl.BlockSpec((tm, LANES), lambda i: (i, 0)),
            pl.BlockSpec((1, d), lambda i: (0, 0)),
            pl.BlockSpec(memory_space=pl.ANY),
        ],
        out_specs=pl.BlockSpec((tm, d), lambda i: (i, 0)),
        out_shape=jax.ShapeDtypeStruct((t, d), F32),
        scratch_shapes=[pltpu.VMEM((TOP_K, tm) + ys.shape[1:], ys.dtype), pltpu.SemaphoreType.DMA],
        compiler_params=_cparams(("arbitrary",)),
        name="combine",
    )(dest_flat, x2, meta, g, ys)


def kernel(x, attn_norm_g, w_in, b_forget, fox_out_g, sb_out_g, w_out, ffn_norm_g, w_router, b_router,
           w_gate_up, b_gate_up, w_down, b_down, final_norm_g):
    b, s, d = x.shape
    t = b * s
    fw = N_HEADS * HEAD_DIM
    x2d = x.reshape(t, d)

    o = 0
    parts = []
    for width in (fw, fw, fw, N_HEADS, fw, fw, fw):
        parts.append(w_in[:, o:o + width])
        o += width
    wqa, wka, wva, wf, wqb, wkb, wvb = parts
    scale = 1.0 / math.sqrt(HEAD_DIM)
    w_all = jnp.concatenate(
        [_pad_heads(wqa * (scale * LOG2E)), _pad_heads(wka), _pad_heads(wva),
         wqb * scale, wkb, _pad_heads(wvb)], axis=1).astype(BF16)
    wf_pad = jnp.zeros((d, LANES), F32).at[:, :N_HEADS].set(wf)
    bf_pad = jnp.zeros((1, LANES), F32).at[0, :N_HEADS].set(b_forget)

    qa, ka, va, qb, kb, vb = _inproj(x2d, attn_norm_g.reshape(1, d), w_all, wf_pad, bf_pad, s)
    ya = _attention_call(_fox_kernel, "fox", qa, ka, va, (), s, 2 * LANES, 2 * LANES,
                         scratch=(pltpu.VMEM((8, LANES), F32),))
    tri = (jnp.arange(SUB)[:, None] >= jnp.arange(SUB)[None, :]).astype(BF16)
    yb = _attention_call(_sb_kernel, "sb", qb, kb, vb, (tri,), s, LANES, LANES)

    wr = jnp.zeros((d, LANES), F32).at[:, :N_EXPERTS].set(w_router)
    br = jnp.full((1, LANES), NEG_BIG, F32).at[0, :N_EXPERTS].set(b_router)
    x2, h2, meta, cnt = _outproj(x2d, ya, yb, fox_out_g.reshape(1, fw), sb_out_g.reshape(1, fw),
                                 w_out.astype(BF16), ffn_norm_g.reshape(1, d), wr, br)

    bm = BM_MOE
    n_blocks = (t * TOP_K + N_EXPERTS * (bm - 1)) // bm
    counts = cnt[0, :N_EXPERTS].astype(jnp.int32)
    padded = (counts + bm - 1) // bm * bm
    ends = jnp.cumsum(padded)
    starts = ends - padded
    top_idx = meta[:, 0:TOP_K].astype(jnp.int32)
    rank = meta[:, TOP_K:2 * TOP_K].astype(jnp.int32)
    dest = (starts[top_idx] + rank).reshape(t * TOP_K)
    block_start = jnp.arange(n_blocks, dtype=jnp.int32) * bm
    block_e = jnp.minimum(jnp.sum(block_start[:, None] >= ends[None, :], axis=1), N_EXPERTS - 1).astype(jnp.int32)
    first = jnp.concatenate([jnp.ones((1,), jnp.int32), (block_e[1:] != block_e[:-1]).astype(jnp.int32)])
    n_valid = (ends[-1:] // bm).astype(jnp.int32)

    xs = _dispatch(dest, h2, n_blocks * bm)
    dff = w_down.shape[1]
    ys = _moe(block_e, first, n_valid, xs, w_gate_up, w_down, b_gate_up.reshape(N_EXPERTS, 1, 2 * dff),
              b_down.reshape(N_EXPERTS, 1, d))
    out = _combine(dest, x2, meta, final_norm_g.reshape(1, d), ys)
    return out.reshape(b, s, d)
```

```python
import functools
import math

import jax
import jax.numpy as jnp
from jax import lax
from jax.experimental import pallas as pl
from jax.experimental.pallas import tpu as pltpu

F32 = jnp.float32
BF16 = jnp.bfloat16

HEAD_DIM = 64
N_HEADS = 8
N_EXPERTS = 32
TOP_K = 4
LANES = 128
NORM_EPS = 1e-5
SWIGLU_LIMIT = 7.0
SWIGLU_ALPHA = 1.702
LOG2E = math.log2(math.e)
NEG_BIG = -1e30
N_BIAS_PARTS = 3
EXP_UNDERFLOW = 110.0
EXP2_UNDERFLOW = 160.0

TM_PROJ = 512
BQ = 512
BK = 512
SUB = 256
TM_ROUTE = 512
ROWMAP_CHUNK = 4096
TM_COMBINE = 256
BM_MOE = 256
VMEM_LIMIT = 56 * 1024 * 1024


def _cparams(sem):
    return pltpu.CompilerParams(dimension_semantics=sem, vmem_limit_bytes=VMEM_LIMIT)


def _bdot(a, b):
    return jnp.dot(a, b, preferred_element_type=F32)


def _split_bf16(x, parts):
    out = []
    for _ in range(parts):
        piece = x.astype(BF16)
        out.append(piece)
        x = x - piece.astype(F32)
    return out


def _dot3(a_hi, a_lo, b_hi_ref, b_lo_ref):
    return _bdot(a_hi, b_hi_ref[...]) + _bdot(a_lo, b_hi_ref[...]) + _bdot(a_hi, b_lo_ref[...])


def _nt_dot(a, b):
    return lax.dot_general(a, b, (((1,), (1,)), ((), ())), preferred_element_type=F32)


def _inproj_kernel(x_ref, g_ref, w_ref, wf_hi_ref, wf_lo_ref, bf_ref, tri_ref, place_ref,
                   qa_ref, ka_ref, va_ref, qb_ref, kb_ref, vb_ref, carry_ref, *, tiles_per_seq):
    i = pl.program_id(0)

    @pl.when(i % tiles_per_seq == 0)
    def _():
        carry_ref[...] = jnp.zeros_like(carry_ref)

    x = x_ref[...]
    ms = jnp.mean(x * x, axis=-1, keepdims=True)
    h = x * lax.rsqrt(ms + NORM_EPS) * g_ref[...]
    hb, h_lo = _split_bf16(h, 2)

    logit = _dot3(hb, h_lo, wf_hi_ref, wf_lo_ref) + bf_ref[...]
    logf = jnp.minimum(logit, 0.0) - jnp.log(1.0 + jnp.exp(-jnp.abs(logit)))
    c = carry_ref[0:1, :]
    for piece in _split_bf16(logf, 3):
        c = c + _bdot(tri_ref[...], piece)
    tm = c.shape[0]
    carry_ref[...] = jnp.broadcast_to(c[tm - 1:tm, :], carry_ref.shape)
    placed = jnp.zeros((tm, ka_ref.shape[1]), F32)
    for n, part in enumerate(_split_bf16(-LOG2E * c, N_BIAS_PARTS)):
        placed = placed + _bdot(part, place_ref[n])

    lane = lax.broadcasted_iota(jnp.int32, (1, qa_ref.shape[1]), 1) % LANES
    q_extra = jnp.where((lane >= HEAD_DIM) & (lane < HEAD_DIM + N_BIAS_PARTS), 1.0, 0.0)
    v_extra = jnp.where(lane >= HEAD_DIM, 1.0, 0.0)
    extras = (q_extra, placed, v_extra, None, None, None)
    offset = 0
    for ref, extra in zip((qa_ref, ka_ref, va_ref, qb_ref, kb_ref, vb_ref), extras):
        width = ref.shape[1]
        out = _bdot(hb, w_ref[:, offset:offset + width])
        ref[...] = (out if extra is None else out + extra).astype(BF16)
        offset += width


def _pad_heads(w):
    d = w.shape[0]
    w = w.reshape(d, N_HEADS, HEAD_DIM)
    return jnp.concatenate([w, jnp.zeros_like(w)], axis=2).reshape(d, N_HEADS * LANES)


def _inproj(x2d, g, w_all, wf, bf, seq):
    t, d = x2d.shape
    tm = TM_PROJ
    width = N_HEADS * LANES
    tiles_per_seq = seq // tm
    tri = (jnp.arange(tm)[:, None] >= jnp.arange(tm)[None, :]).astype(BF16)
    wf_hi = wf.astype(BF16)
    wf_lo = (wf - wf_hi.astype(F32)).astype(BF16)
    src = jnp.arange(LANES)[:, None]
    dst = jnp.arange(width)[None, :]
    place = jnp.stack([((dst == src * LANES + HEAD_DIM + n) & (src < N_HEADS)).astype(BF16)
                       for n in range(N_BIAS_PARTS)])
    pair_width = N_HEADS * HEAD_DIM
    widths = (width, width, width, pair_width, pair_width, width)
    assert sum(widths) == w_all.shape[1]
    whole = lambda a: pl.BlockSpec(a.shape, lambda i: (0,) * a.ndim)
    return pl.pallas_call(
        functools.partial(_inproj_kernel, tiles_per_seq=tiles_per_seq),
        grid=(t // tm,),
        in_specs=[pl.BlockSpec((tm, d), lambda i: (i, 0)), whole(g), whole(w_all), whole(wf_hi), whole(wf_lo),
                  whole(bf), whole(tri), whole(place)],
        out_specs=[pl.BlockSpec((tm, w), lambda i: (i, 0)) for w in widths],
        out_shape=[jax.ShapeDtypeStruct((t, w), BF16) for w in widths],
        scratch_shapes=[pltpu.VMEM((8, LANES), F32)],
        compiler_params=_cparams(("arbitrary",)),
        name="inproj",
    )(x2d, g, w_all, wf_hi, wf_lo, bf, tri, place)


def _pair_out(o_even, o_odd):
    lane = lax.broadcasted_iota(jnp.int32, o_even.shape, 1)
    return jnp.where(lane < HEAD_DIM, o_even, pltpu.roll(o_odd, HEAD_DIM, axis=1))


def _head_sq_norm(tile):
    tile = tile.astype(F32)
    lane = lax.broadcasted_iota(jnp.int32, tile.shape, 1)
    return jnp.sum(jnp.where(lane < HEAD_DIM, tile * tile, 0.0), axis=1, keepdims=True)


def _fox_kernel(q_ref, k_ref, v_ref, o_ref, kmax_ref):
    i = pl.program_id(2)
    bq = q_ref.shape[0]
    n_seq_blocks = k_ref.shape[0] // BK

    @pl.when(i == 0)
    def _():
        def body(n, best):
            start = pl.multiple_of(n * BK, BK)
            return tuple(
                jnp.maximum(best[hh], jnp.max(_head_sq_norm(k_ref[pl.ds(start, BK), hh * LANES:(hh + 1) * LANES]),
                                              axis=0, keepdims=True))
                for hh in range(2))
        best = lax.fori_loop(0, n_seq_blocks, body, (jnp.zeros((1, 1), F32), jnp.zeros((1, 1), F32)))
        for hh in range(2):
            kmax_ref[hh:hh + 1, :] = jnp.broadcast_to(jnp.sqrt(best[hh]), (1, LANES))

    q_heads = (q_ref[:, 0:LANES], q_ref[:, LANES:2 * LANES])
    reach = tuple(jnp.sqrt(_head_sq_norm(q_heads[hh])) * kmax_ref[hh:hh + 1, 0:1] for hh in range(2))
    row = lax.broadcasted_iota(jnp.int32, (bq, BK), 0)
    col = lax.broadcasted_iota(jnp.int32, (bq, BK), 1)
    causal = col <= row
    lane1 = lax.broadcasted_iota(jnp.int32, (1, LANES), 1)
    bias_lanes = (lane1 >= HEAD_DIM) & (lane1 < HEAD_DIM + N_BIAS_PARTS)

    def step(j, carry, diag):
        start = pl.multiple_of(j * BK, BK)
        new = []
        slack = None
        for hh in range(2):
            m, acc = carry[hh]
            ks = k_ref[pl.ds(start, BK), hh * LANES:(hh + 1) * LANES]
            vs = v_ref[pl.ds(start, BK), hh * LANES:(hh + 1) * LANES]
            s = _nt_dot(q_heads[hh], ks)
            if diag:
                s = jnp.where(causal, s, NEG_BIG)
            m_new = jnp.maximum(m, jnp.max(s, axis=1, keepdims=True))
            p = jnp.exp2(s - m_new)
            acc = jnp.exp2(m - m_new) * acc + _bdot(p.astype(BF16), vs)
            new.append((m_new, acc))
            first = k_ref[pl.ds(start, 1), hh * LANES:(hh + 1) * LANES].astype(F32)
            bias0 = jnp.sum(jnp.where(bias_lanes, first, 0.0), axis=1, keepdims=True)
            bound = jnp.max(reach[hh] + bias0 - m_new)
            slack = bound if slack is None else jnp.maximum(slack, bound)
        return tuple(new), (slack < -EXP2_UNDERFLOW).astype(jnp.int32)

    init = tuple((jnp.full((bq, 1), NEG_BIG, F32), jnp.zeros((bq, LANES), F32)) for _ in range(2))
    carry, done = step(i, init, True)

    def block(state):
        n, _, c = state
        c, done = step(i - 1 - n, c, False)
        return n + 1, done, c

    _, _, carry = lax.while_loop(lambda st: (st[0] < i) & (st[1] == 0), block, (jnp.int32(0), done, carry))
    (_, acc_a), (_, acc_b) = carry
    norm = lambda acc: acc / pltpu.roll(acc, HEAD_DIM, axis=1)
    o_ref[...] = _pair_out(norm(acc_a), norm(acc_b))


def _softplus(z):
    return jnp.maximum(z, 0.0) + jnp.log(1.0 + jnp.exp(-jnp.abs(z)))


def _sb_kernel(q_ref, k_ref, v_ref, tri_ref, o_ref):
    i = pl.program_id(2)
    bq = q_ref.shape[0]
    n_sub = bq // SUB
    q2 = q_ref[...]
    lane_q = lax.broadcasted_iota(jnp.int32, q2.shape, 1)
    q_heads = (jnp.where(lane_q < HEAD_DIM, q2, jnp.zeros_like(q2)),
               jnp.where(lane_q >= HEAD_DIM, q2, jnp.zeros_like(q2)))
    row = lax.broadcasted_iota(jnp.int32, (SUB, SUB), 0)
    col = lax.broadcasted_iota(jnp.int32, (SUB, SUB), 1)
    strict = col < row
    tri = tri_ref[...]

    def sub_step(start, carry, first_row):
        r0 = 0 if first_row is None else first_row

        def mask_top(x):
            top = jnp.where(strict, x[:SUB], 0.0)
            return top if x.shape[0] == SUB else jnp.concatenate([top, x[SUB:]], axis=0)

        def add_rows(full, part):
            return full + part if r0 == 0 else jnp.concatenate([full[:r0], full[r0:] + part], axis=0)

        ks = k_ref[pl.ds(start, SUB), :]
        new = []
        for hh in range(2):
            later, acc = carry[hh]
            vs = v_ref[pl.ds(start, SUB), hh * LANES:(hh + 1) * LANES]
            z = _nt_dot(q_heads[hh][r0:], ks)
            sp = _softplus(z)
            if first_row is not None:
                sp = mask_top(sp)
            hi = sp.astype(BF16)
            lo = (sp - hi.astype(F32)).astype(BF16)
            g = _bdot(hi, tri) + _bdot(lo, tri)
            a = jnp.exp(z - g - later[r0:])
            if first_row is not None:
                a = mask_top(a)
            new.append((add_rows(later, g[:, 0:1]), add_rows(acc, _bdot(a.astype(BF16), vs))))
        return tuple(new)

    carry = tuple((jnp.zeros((bq, 1), F32), jnp.zeros((bq, LANES), F32)) for _ in range(2))
    for u in reversed(range(n_sub)):
        carry = sub_step(pl.multiple_of(i * bq + u * SUB, SUB), carry, u * SUB)

    def decayed(c):
        return (jnp.min(jnp.minimum(c[0][0], c[1][0])) >= EXP_UNDERFLOW).astype(jnp.int32)

    def back(state):
        n, _, c = state
        c = sub_step(pl.multiple_of(i * bq - (n + 1) * SUB, SUB), c, None)
        return n + 1, decayed(c), c

    _, _, carry = lax.while_loop(lambda st: (st[0] < i * n_sub) & (st[1] == 0), back,
                                 (jnp.int32(0), decayed(carry), carry))
    o_ref[...] = _pair_out(carry[0][1], carry[1][1])


def _attention_call(body, name, q, k, v, extra, seq, q_lanes, k_lanes, scratch=()):
    t = q.shape[0]
    nb, nq, npair = t // seq, seq // BQ, N_HEADS // 2
    q_spec = lambda w: pl.BlockSpec((BQ, w), lambda b, p, i: (b * nq + i, p))
    kv_spec = lambda w: pl.BlockSpec((seq, w), lambda b, p, i: (b, p))
    return pl.pallas_call(
        body,
        grid=(nb, npair, nq),
        in_specs=[q_spec(q_lanes), kv_spec(k_lanes), kv_spec(2 * LANES)]
        + [pl.BlockSpec(a.shape, lambda b, p, i: (0,) * a.ndim) for a in extra],
        out_specs=q_spec(LANES),
        out_shape=jax.ShapeDtypeStruct((t, npair * LANES), F32),
        scratch_shapes=list(scratch),
        compiler_params=_cparams(("arbitrary", "arbitrary", "arbitrary")),
        name=name,
    )(q, k, v, *extra)


def _rms(y, g):
    return y * lax.rsqrt(jnp.mean(y * y, axis=-1, keepdims=True) + NORM_EPS) * g


def _as_u32(words):
    return words if words.dtype == jnp.uint32 else pltpu.bitcast(words, jnp.uint32)


def _pack_bf16_pairs(x):
    n = x.shape[1] // 2
    return _as_u32(pltpu.pack_elementwise([x[:, :n], x[:, n:]], packed_dtype=BF16))


def _unpack_bf16_pairs(p):
    halves = [pltpu.unpack_elementwise(p, index=n, packed_dtype=BF16, unpacked_dtype=F32) for n in range(2)]
    return jnp.concatenate(halves, axis=1)


def _outproj_kernel(x_ref, ya_ref, yb_ref, ga_ref, gb_ref, wo_ref, gf_ref, wr_hi_ref, wr_lo_ref, br_ref,
                    ltri_ref, x2_ref, h2p_ref, meta_ref, cnt_ref, carry_ref):
    i = pl.program_id(0)

    @pl.when(i == 0)
    def _():
        carry_ref[...] = jnp.zeros_like(carry_ref)

    ya = _rms(ya_ref[...], ga_ref[...]).astype(BF16)
    yb = _rms(yb_ref[...], gb_ref[...]).astype(BF16)
    wa = ya_ref.shape[1]
    x2 = x_ref[...] + _bdot(ya, wo_ref[0:wa, :]) + _bdot(yb, wo_ref[wa:, :])
    x2_ref[...] = x2
    h2 = _rms(x2, gf_ref[...])
    h2p_ref[...] = _pack_bf16_pairs(h2)

    h2_hi, h2_lo = _split_bf16(h2, 2)
    logits = _dot3(h2_hi, h2_lo, wr_hi_ref, wr_lo_ref) + br_ref[...]
    tm = logits.shape[0]
    lane = lax.broadcasted_iota(jnp.int32, (tm, LANES), 1)
    lane_f = lane.astype(F32)
    work = logits
    vals, sels, idxs = [], [], []
    for _ in range(TOP_K):
        mx = jnp.max(work, axis=1, keepdims=True)
        idx = jnp.min(jnp.where(work == mx, lane_f, float(LANES)), axis=1, keepdims=True)
        sel = lane_f == idx
        vals.append(mx)
        idxs.append(idx)
        sels.append(sel)
        work = jnp.where(sel, -jnp.inf, work)
    exps = [jnp.exp(v - vals[0]) for v in vals]
    denom = exps[0] + exps[1] + exps[2] + exps[3]
    gates = [e / denom for e in exps]

    onehot = jnp.zeros((tm, LANES), F32)
    for sel in sels:
        onehot = onehot + jnp.where(sel, 1.0, 0.0)
    before = _bdot(ltri_ref[...], onehot.astype(BF16)) + carry_ref[0:1, :]
    meta = jnp.zeros((tm, LANES), F32)
    for k in range(TOP_K):
        rank = jnp.sum(jnp.where(sels[k], before, 0.0), axis=1, keepdims=True)
        meta = jnp.where(lane == k, idxs[k], meta)
        meta = jnp.where(lane == TOP_K + k, rank, meta)
        meta = jnp.where(lane == 2 * TOP_K + k, gates[k], meta)
    meta_ref[...] = meta
    total = carry_ref[0:1, :] + jnp.sum(onehot, axis=0, keepdims=True)
    carry_ref[...] = jnp.broadcast_to(total, carry_ref.shape)
    cnt_ref[...] = jnp.broadcast_to(total, cnt_ref.shape)


def _outproj(x2d, ya, yb, ga, gb, wo, gf, wr, br):
    t, d = x2d.shape
    tm = TM_ROUTE
    wa = ya.shape[1]
    ltri = (jnp.arange(tm)[:, None] > jnp.arange(tm)[None, :]).astype(BF16)
    wr_hi = wr.astype(BF16)
    wr_lo = (wr - wr_hi.astype(F32)).astype(BF16)
    tile = lambda w: pl.BlockSpec((tm, w), lambda i: (i, 0))
    whole = lambda a: pl.BlockSpec(a.shape, lambda i: (0, 0))
    return pl.pallas_call(
        _outproj_kernel,
        grid=(t // tm,),
        in_specs=[tile(d), tile(wa), tile(wa), whole(ga), whole(gb), whole(wo), whole(gf),
                  whole(wr_hi), whole(wr_lo), whole(br), whole(ltri)],
        out_specs=[tile(d), tile(d // 2), tile(LANES), pl.BlockSpec((8, LANES), lambda i: (0, 0))],
        out_shape=[jax.ShapeDtypeStruct((t, d), F32), jax.ShapeDtypeStruct((t, d // 2), jnp.uint32),
                   jax.ShapeDtypeStruct((t, LANES), F32), jax.ShapeDtypeStruct((8, LANES), F32)],
        scratch_shapes=[pltpu.VMEM((8, LANES), F32)],
        compiler_params=_cparams(("arbitrary",)),
        name="outproj",
    )(x2d, ya, yb, ga, gb, wo, gf, wr_hi, wr_lo, br, ltri)


def _rowmap_kernel(fill_lo_ref, fill_hi_ref, dest_ref, map_ref, *, n_slots, bm):
    n = dest_ref.shape[0]
    step = pl.program_id(0)

    @pl.when(step == 0)
    def _():
        def fill_range(e, _):
            def fill(j, _):
                map_ref[j] = n_slots + (j & (bm - 1))
                return 0
            return lax.fori_loop(fill_lo_ref[e], fill_hi_ref[e], fill, 0)
        lax.fori_loop(0, fill_lo_ref.shape[0], fill_range, 0)

    def body(j, _):
        map_ref[dest_ref[j]] = step * n + j
        return 0

    lax.fori_loop(0, n, body, 0, unroll=8)


def _rowmap(dest_flat, fill_lo, fill_hi, n_rows):
    n = dest_flat.shape[0]
    chunk = ROWMAP_CHUNK
    assert BM_MOE & (BM_MOE - 1) == 0
    grid_spec = pltpu.PrefetchScalarGridSpec(
        num_scalar_prefetch=2,
        grid=(n // chunk,),
        in_specs=[pl.BlockSpec((chunk,), lambda i, *_: (i,), memory_space=pltpu.SMEM)],
        out_specs=pl.BlockSpec((n_rows,), lambda i, *_: (0,), memory_space=pltpu.SMEM),
    )
    return pl.pallas_call(
        functools.partial(_rowmap_kernel, n_slots=n, bm=BM_MOE),
        grid_spec=grid_spec,
        out_shape=jax.ShapeDtypeStruct((n_rows,), jnp.int32),
        compiler_params=_cparams(("arbitrary",)),
        name="rowmap",
    )(fill_lo, fill_hi, dest_flat)


def _moe_kernel(be_ref, first_ref, nv_ref, sprev_ref, snext_ref, h_ref, wgu_ref, wd_ref, bgu_ref, bd_ref,
                ysg_ref, wgu_bf_ref, wd2_ref, x0_ref, x1_ref, y0_ref, y1_ref, gsem, ssem):
    del be_ref
    i = pl.program_id(0)
    nv = nv_ref[0]
    bm = x0_ref.shape[0]
    n_tokens = h_ref.shape[0]
    xbufs, ybufs = (x0_ref, x1_ref), (y0_ref, y1_ref)

    def gather_row(slots_ref, r, buf):
        token = slots_ref[r] & (n_tokens - 1)
        return pltpu.make_async_copy(h_ref.at[pl.ds(token, 1), :], xbufs[buf].at[pl.ds(r, 1), :], gsem.at[buf])

    def scatter_row(slots_ref, r, buf):
        return pltpu.make_async_copy(ybufs[buf].at[pl.ds(r, 1), :], ysg_ref.at[pl.ds(slots_ref[r], 1), :],
                                     ssem.at[buf])

    @pl.when(i == 0)
    def _():
        y1_ref[...] = jnp.zeros_like(y1_ref)
        spare = pltpu.make_async_copy(y1_ref, ysg_ref.at[pl.ds(ysg_ref.shape[0] - bm, bm), :], ssem.at[1])
        spare.start()
        spare.wait()
        for r in range(bm):
            gather_row(sprev_ref, r, 0).start()

    @pl.when(first_ref[i] == 1)
    def _():
        wgu_bf_ref[...] = wgu_ref[0].astype(BF16)
        wd = wd_ref[0]
        wd2_ref[...] = _as_u32(pltpu.pack_elementwise([wd, wd], packed_dtype=BF16))

    def compute(cur):
        nxt = 1 - cur
        for r in range(bm):
            gather_row(snext_ref, r, cur).wait()
        xb = _unpack_bf16_pairs(xbufs[cur][...]).astype(BF16)
        for r in range(bm):
            scatter_row(sprev_ref, r, nxt).start()
            gather_row(snext_ref, r, nxt).start()
        gu = _bdot(xb, wgu_bf_ref[...]) + bgu_ref[0]
        even = lax.broadcasted_iota(jnp.int32, (gu.shape[0], LANES), 1) % 2 == 0
        acts = []
        for c in range(gu.shape[1] // LANES):
            blk = gu[:, c * LANES:(c + 1) * LANES]
            gate = jnp.minimum(blk, SWIGLU_LIMIT)
            glu = gate * (1.0 / (1.0 + jnp.exp(-SWIGLU_ALPHA * gate)))
            up1 = jnp.clip(blk, -SWIGLU_LIMIT, SWIGLU_LIMIT) + 1.0
            acts.append(jnp.where(even, glu * pltpu.roll(up1, LANES - 1, axis=1), 0.0).astype(BF16))
        act = jnp.concatenate(acts, axis=1)
        wd2 = pltpu.bitcast(wd2_ref[...], BF16)
        ybufs[cur][...] = _pack_bf16_pairs(_bdot(act, wd2) + bd_ref[0])

    def drain(cur):
        nxt = 1 - cur
        for r in range(bm):
            scatter_row(sprev_ref, r, nxt).start()
        for r in range(bm):
            scatter_row(sprev_ref, r, nxt).wait()
        for r in range(bm):
            gather_row(snext_ref, r, cur).wait()

    for cur in range(2):
        parity = (i % 2) == cur

        @pl.when(parity & (i >= 1) & (i <= nv))
        def _():
            for r in range(bm):
                scatter_row(sprev_ref, r, cur).wait()

        pl.when(parity & (i < nv))(functools.partial(compute, cur))
        pl.when(parity & (i == nv))(functools.partial(drain, cur))


def _moe(block_e, first, n_valid, rowmap, h2p, wgu, wd, bgu, bd):
    t, half = h2p.shape
    assert t & (t - 1) == 0, "token count must be a power of two (slot -> token uses a bit mask)"
    bm = BM_MOE
    n_blocks = rowmap.shape[0] // bm
    dff, d = wd.shape[1:]
    w_spec = lambda a: pl.BlockSpec((1,) + a.shape[1:], lambda i, be, *_: (be[i], 0, 0))
    slots = lambda index: pl.BlockSpec((bm,), index, memory_space=pltpu.SMEM)
    tile = pltpu.VMEM((bm, half), jnp.uint32)
    grid_spec = pltpu.PrefetchScalarGridSpec(
        num_scalar_prefetch=3,
        grid=(n_blocks + 1,),
        in_specs=[slots(lambda i, *_: (jnp.maximum(i - 1, 0),)),
                  slots(lambda i, *_: (jnp.minimum(i + 1, n_blocks - 1),)),
                  pl.BlockSpec(memory_space=pl.ANY),
                  w_spec(wgu), w_spec(wd), w_spec(bgu), w_spec(bd)],
        out_specs=pl.BlockSpec(memory_space=pl.ANY),
        scratch_shapes=[pltpu.VMEM((d, 2 * dff), BF16), pltpu.VMEM((dff, d), jnp.uint32),
                        tile, tile, tile, tile,
                        pltpu.SemaphoreType.DMA((2,)), pltpu.SemaphoreType.DMA((2,))],
    )
    return pl.pallas_call(
        _moe_kernel,
        grid_spec=grid_spec,
        out_shape=jax.ShapeDtypeStruct((TOP_K * t + bm, half), jnp.uint32),
        compiler_params=_cparams(("arbitrary",)),
        name="moe",
    )(block_e, first, n_valid, rowmap, rowmap, h2p, wgu, wd, bgu, bd)


def _combine_kernel(x2_ref, meta_ref, g_ref, *rest):
    ys_refs, o_ref = rest[:TOP_K], rest[TOP_K]
    meta = meta_ref[...]
    y = x2_ref[...]
    for k in range(TOP_K):
        y = y + meta[:, 2 * TOP_K + k:2 * TOP_K + k + 1] * _unpack_bf16_pairs(ys_refs[k][...])
    o_ref[...] = _rms(y, g_ref[...])


def _combine(x2, meta, g, ysg):
    t, d = x2.shape
    tm = TM_COMBINE
    tiles = t // tm
    slot_spec = lambda k: pl.BlockSpec((tm, ysg.shape[1]), lambda i: (k * tiles + i, 0))
    return pl.pallas_call(
        _combine_kernel,
        grid=(tiles,),
        in_specs=[pl.BlockSpec((tm, d), lambda i: (i, 0)),
                  pl.BlockSpec((tm, LANES), lambda i: (i, 0)),
                  pl.BlockSpec((1, d), lambda i: (0, 0))] + [slot_spec(k) for k in range(TOP_K)],
        out_specs=pl.BlockSpec((tm, d), lambda i: (i, 0)),
        out_shape=jax.ShapeDtypeStruct((t, d), F32),
        compiler_params=_cparams(("arbitrary",)),
        name="combine",
    )(x2, meta, g, *([ysg] * TOP_K))


def kernel(x, attn_norm_g, w_in, b_forget, fox_out_g, sb_out_g, w_out, ffn_norm_g, w_router, b_router,
           w_gate_up, b_gate_up, w_down, b_down, final_norm_g):
    b, s, d = x.shape
    t = b * s
    fw = N_HEADS * HEAD_DIM
    x2d = x.reshape(t, d)

    o = 0
    parts = []
    for width in (fw, fw, fw, N_HEADS, fw, fw, fw):
        parts.append(w_in[:, o:o + width])
        o += width
    wqa, wka, wva, wf, wqb, wkb, wvb = parts
    scale = 1.0 / math.sqrt(HEAD_DIM)
    w_all = jnp.concatenate(
        [_pad_heads(wqa * (scale * LOG2E)), _pad_heads(wka), _pad_heads(wva),
         wqb * scale, wkb, _pad_heads(wvb)], axis=1).astype(BF16)
    wf_pad = jnp.zeros((d, LANES), F32).at[:, :N_HEADS].set(wf)
    bf_pad = jnp.zeros((1, LANES), F32).at[0, :N_HEADS].set(b_forget)

    qa, ka, va, qb, kb, vb = _inproj(x2d, attn_norm_g.reshape(1, d), w_all, wf_pad, bf_pad, s)
    ya = _attention_call(_fox_kernel, "fox", qa, ka, va, (), s, 2 * LANES, 2 * LANES,
                         scratch=(pltpu.VMEM((8, LANES), F32),))
    tri = (jnp.arange(SUB)[:, None] >= jnp.arange(SUB)[None, :]).astype(BF16)
    yb = _attention_call(_sb_kernel, "sb", qb, kb, vb, (tri,), s, LANES, LANES)

    wr = jnp.zeros((d, LANES), F32).at[:, :N_EXPERTS].set(w_router)
    br = jnp.full((1, LANES), NEG_BIG, F32).at[0, :N_EXPERTS].set(b_router)
    x2, h2p, meta, cnt = _outproj(x2d, ya, yb, fox_out_g.reshape(1, fw), sb_out_g.reshape(1, fw),
                                 w_out.astype(BF16), ffn_norm_g.reshape(1, d), wr, br)

    bm = BM_MOE
    n_blocks = (t * TOP_K + N_EXPERTS * (bm - 1)) // bm
    counts = cnt[0, :N_EXPERTS].astype(jnp.int32)
    padded = (counts + bm - 1) // bm * bm
    ends = jnp.cumsum(padded)
    starts = ends - padded
    top_idx = meta[:, 0:TOP_K].astype(jnp.int32)
    rank = meta[:, TOP_K:2 * TOP_K].astype(jnp.int32)
    dest = (starts[top_idx] + rank).T.reshape(t * TOP_K)
    block_start = jnp.arange(n_blocks + 1, dtype=jnp.int32) * bm
    block_e = jnp.minimum(jnp.sum(block_start[:, None] >= ends[None, :], axis=1), N_EXPERTS - 1).astype(jnp.int32)
    first = jnp.concatenate([jnp.ones((1,), jnp.int32), (block_e[1:] != block_e[:-1]).astype(jnp.int32)])
    n_valid = (ends[-1:] // bm).astype(jnp.int32)

    n_rows = n_blocks * bm
    fill_lo = jnp.concatenate([starts + counts, ends[-1:]]).astype(jnp.int32)
    fill_hi = jnp.concatenate([ends, jnp.full((1,), n_rows, jnp.int32)]).astype(jnp.int32)
    rowmap = _rowmap(dest, fill_lo, fill_hi, n_rows)
    dff = w_down.shape[1]
    ysg = _moe(block_e, first, n_valid, rowmap, h2p, w_gate_up, w_down,
               b_gate_up.reshape(N_EXPERTS, 1, 2 * dff), b_down.reshape(N_EXPERTS, 1, d))
    out = _combine(x2, meta, final_norm_g.reshape(1, d), ysg)
    return out.reshape(b, s, d)
```

```python
import functools
import math

import jax
import jax.numpy as jnp
from jax import lax
from jax.experimental import pallas as pl
from jax.experimental.pallas import tpu as pltpu

F32 = jnp.float32
BF16 = jnp.bfloat16

HEAD_DIM = 64
N_HEADS = 8
N_EXPERTS = 32
TOP_K = 4
LANES = 128
NORM_EPS = 1e-5
SWIGLU_LIMIT = 7.0
SWIGLU_ALPHA = 1.702
LOG2E = math.log2(math.e)
NEG_BIG = -1e30
N_BIAS_PARTS = 3
EXP_UNDERFLOW = 110.0
EXP2_UNDERFLOW = 160.0

TM_PROJ = 512
BQ = 512
BK = 512
SUB = 256
TM_ROUTE = 512
ROWMAP_CHUNK = 4096
TM_COMBINE = 256
BM_MOE = 256
VMEM_LIMIT = 56 * 1024 * 1024


def _cparams(sem):
    return pltpu.CompilerParams(dimension_semantics=sem, vmem_limit_bytes=VMEM_LIMIT)


def _bdot(a, b):
    return jnp.dot(a, b, preferred_element_type=F32)


def _split_bf16(x, parts):
    out = []
    for _ in range(parts):
        piece = x.astype(BF16)
        out.append(piece)
        x = x - piece.astype(F32)
    return out


def _dot3(a_hi, a_lo, b_hi_ref, b_lo_ref):
    return _bdot(a_hi, b_hi_ref[...]) + _bdot(a_lo, b_hi_ref[...]) + _bdot(a_hi, b_lo_ref[...])


def _nt_dot(a, b):
    return lax.dot_general(a, b, (((1,), (1,)), ((), ())), preferred_element_type=F32)


def _inproj_kernel(x_ref, g_ref, w_ref, wf_hi_ref, wf_lo_ref, bf_ref, tri_ref, place_ref,
                   qa_ref, ka_ref, va_ref, qb_ref, kb_ref, vb_ref, carry_ref, *, tiles_per_seq):
    i = pl.program_id(0)

    @pl.when(i % tiles_per_seq == 0)
    def _():
        carry_ref[...] = jnp.zeros_like(carry_ref)

    x = x_ref[...]
    ms = jnp.mean(x * x, axis=-1, keepdims=True)
    h = x * lax.rsqrt(ms + NORM_EPS) * g_ref[...]
    hb, h_lo = _split_bf16(h, 2)

    logit = _dot3(hb, h_lo, wf_hi_ref, wf_lo_ref) + bf_ref[...]
    logf = jnp.minimum(logit, 0.0) - jnp.log(1.0 + jnp.exp(-jnp.abs(logit)))
    c = carry_ref[0:1, :]
    for piece in _split_bf16(logf, 3):
        c = c + _bdot(tri_ref[...], piece)
    tm = c.shape[0]
    carry_ref[...] = jnp.broadcast_to(c[tm - 1:tm, :], carry_ref.shape)
    placed = jnp.zeros((tm, ka_ref.shape[1]), F32)
    for n, part in enumerate(_split_bf16(-LOG2E * c, N_BIAS_PARTS)):
        placed = placed + _bdot(part, place_ref[n])

    lane = lax.broadcasted_iota(jnp.int32, (1, qa_ref.shape[1]), 1) % LANES
    q_extra = jnp.where((lane >= HEAD_DIM) & (lane < HEAD_DIM + N_BIAS_PARTS), 1.0, 0.0)
    v_extra = jnp.where(lane >= HEAD_DIM, 1.0, 0.0)
    extras = (q_extra, placed, v_extra, None, None, None)
    offset = 0
    for ref, extra in zip((qa_ref, ka_ref, va_ref, qb_ref, kb_ref, vb_ref), extras):
        width = ref.shape[1]
        out = _bdot(hb, w_ref[:, offset:offset + width])
        ref[...] = (out if extra is None else out + extra).astype(BF16)
        offset += width


def _pad_heads(w):
    d = w.shape[0]
    w = w.reshape(d, N_HEADS, HEAD_DIM)
    return jnp.concatenate([w, jnp.zeros_like(w)], axis=2).reshape(d, N_HEADS * LANES)


def _inproj(x2d, g, w_all, wf, bf, seq):
    t, d = x2d.shape
    tm = TM_PROJ
    width = N_HEADS * LANES
    tiles_per_seq = seq // tm
    tri = (jnp.arange(tm)[:, None] >= jnp.arange(tm)[None, :]).astype(BF16)
    wf_hi = wf.astype(BF16)
    wf_lo = (wf - wf_hi.astype(F32)).astype(BF16)
    src = jnp.arange(LANES)[:, None]
    dst = jnp.arange(width)[None, :]
    place = jnp.stack([((dst == src * LANES + HEAD_DIM + n) & (src < N_HEADS)).astype(BF16)
                       for n in range(N_BIAS_PARTS)])
    pair_width = N_HEADS * HEAD_DIM
    widths = (width, width, width, pair_width, pair_width, width)
    assert sum(widths) == w_all.shape[1]
    whole = lambda a: pl.BlockSpec(a.shape, lambda i: (0,) * a.ndim)
    return pl.pallas_call(
        functools.partial(_inproj_kernel, tiles_per_seq=tiles_per_seq),
        grid=(t // tm,),
        in_specs=[pl.BlockSpec((tm, d), lambda i: (i, 0)), whole(g), whole(w_all), whole(wf_hi), whole(wf_lo),
                  whole(bf), whole(tri), whole(place)],
        out_specs=[pl.BlockSpec((tm, w), lambda i: (i, 0)) for w in widths],
        out_shape=[jax.ShapeDtypeStruct((t, w), BF16) for w in widths],
        scratch_shapes=[pltpu.VMEM((8, LANES), F32)],
        compiler_params=_cparams(("arbitrary",)),
        name="inproj",
    )(x2d, g, w_all, wf_hi, wf_lo, bf, tri, place)


def _pair_out(o_even, o_odd):
    lane = lax.broadcasted_iota(jnp.int32, o_even.shape, 1)
    return jnp.where(lane < HEAD_DIM, o_even, pltpu.roll(o_odd, HEAD_DIM, axis=1))


def _head_sq_norm(tile):
    tile = tile.astype(F32)
    lane = lax.broadcasted_iota(jnp.int32, tile.shape, 1)
    return jnp.sum(jnp.where(lane < HEAD_DIM, tile * tile, 0.0), axis=1, keepdims=True)


def _fox_kernel(q_ref, k_ref, v_ref, o_ref, kmax_ref, s_ref):
    i = pl.program_id(2)
    bq = q_ref.shape[0]
    n_seq_blocks = k_ref.shape[0] // BK

    @pl.when(i == 0)
    def _():
        def body(n, best):
            start = pl.multiple_of(n * BK, BK)
            return tuple(
                jnp.maximum(best[hh], jnp.max(_head_sq_norm(k_ref[pl.ds(start, BK), hh * LANES:(hh + 1) * LANES]),
                                              axis=0, keepdims=True))
                for hh in range(2))
        best = lax.fori_loop(0, n_seq_blocks, body, (jnp.zeros((1, 1), F32), jnp.zeros((1, 1), F32)))
        for hh in range(2):
            kmax_ref[hh:hh + 1, :] = jnp.broadcast_to(jnp.sqrt(best[hh]), (1, LANES))

    q_heads = (q_ref[:, 0:LANES], q_ref[:, LANES:2 * LANES])
    reach = tuple(jnp.sqrt(_head_sq_norm(q_heads[hh])) * kmax_ref[hh:hh + 1, 0:1] for hh in range(2))
    row = lax.broadcasted_iota(jnp.int32, (bq, BK), 0)
    col = lax.broadcasted_iota(jnp.int32, (bq, BK), 1)
    causal = col <= row
    lane1 = lax.broadcasted_iota(jnp.int32, (1, LANES), 1)
    bias_lanes = (lane1 >= HEAD_DIM) & (lane1 < HEAD_DIM + N_BIAS_PARTS)

    def scores(j, hh):
        start = pl.multiple_of(j * BK, BK)
        return _nt_dot(q_heads[hh], k_ref[pl.ds(start, BK), hh * LANES:(hh + 1) * LANES])

    def look_ahead(j):
        nxt = jnp.maximum(j - 1, 0)
        return [scores(nxt, hh) for hh in range(2)]

    def step(j, carry, s_pair):
        start = pl.multiple_of(j * BK, BK)
        new = []
        slack = None
        for hh in range(2):
            m, acc = carry[hh]
            s = s_pair[hh]
            vs = v_ref[pl.ds(start, BK), hh * LANES:(hh + 1) * LANES]
            m_new = jnp.maximum(m, jnp.max(s, axis=1, keepdims=True))
            p = jnp.exp2(s - m_new)
            acc = jnp.exp2(m - m_new) * acc + _bdot(p.astype(BF16), vs)
            new.append((m_new, acc))
            first = k_ref[pl.ds(start, 1), hh * LANES:(hh + 1) * LANES].astype(F32)
            bias0 = jnp.sum(jnp.where(bias_lanes, first, 0.0), axis=1, keepdims=True)
            bound = jnp.max(reach[hh] + bias0 - m_new)
            slack = bound if slack is None else jnp.maximum(slack, bound)
        return tuple(new), (slack < -EXP2_UNDERFLOW).astype(jnp.int32)

    init = tuple((jnp.full((bq, 1), NEG_BIG, F32), jnp.zeros((bq, LANES), F32)) for _ in range(2))
    ahead = look_ahead(i)
    carry, done = step(i, init, [jnp.where(causal, scores(i, hh), NEG_BIG) for hh in range(2)])
    for hh in range(2):
        s_ref[hh] = ahead[hh]

    def block(state):
        n, _, c = state
        j = i - 1 - n
        s_pair = [s_ref[hh] for hh in range(2)]
        ahead = look_ahead(j)
        c, done = step(j, c, s_pair)
        for hh in range(2):
            s_ref[hh] = ahead[hh]
        return n + 1, done, c

    _, _, carry = lax.while_loop(lambda st: (st[0] < i) & (st[1] == 0), block, (jnp.int32(0), done, carry))
    (_, acc_a), (_, acc_b) = carry
    norm = lambda acc: acc / pltpu.roll(acc, HEAD_DIM, axis=1)
    o_ref[...] = _pair_out(norm(acc_a), norm(acc_b))


def _softplus(z):
    return jnp.maximum(z, 0.0) + jnp.log(1.0 + jnp.exp(-jnp.abs(z)))


def _sb_kernel(q_ref, k_ref, v_ref, tri_ref, o_ref):
    i = pl.program_id(2)
    bq = q_ref.shape[0]
    n_sub = bq // SUB
    q2 = q_ref[...]
    lane_q = lax.broadcasted_iota(jnp.int32, q2.shape, 1)
    q_heads = (jnp.where(lane_q < HEAD_DIM, q2, jnp.zeros_like(q2)),
               jnp.where(lane_q >= HEAD_DIM, q2, jnp.zeros_like(q2)))
    row = lax.broadcasted_iota(jnp.int32, (SUB, SUB), 0)
    col = lax.broadcasted_iota(jnp.int32, (SUB, SUB), 1)
    strict = col < row
    tri = tri_ref[...]

    def sub_step(start, carry, first_row):
        r0 = 0 if first_row is None else first_row

        def mask_top(x):
            top = jnp.where(strict, x[:SUB], 0.0)
            return top if x.shape[0] == SUB else jnp.concatenate([top, x[SUB:]], axis=0)

        def add_rows(full, part):
            return full + part if r0 == 0 else jnp.concatenate([full[:r0], full[r0:] + part], axis=0)

        ks = k_ref[pl.ds(start, SUB), :]
        new = []
        for hh in range(2):
            later, acc = carry[hh]
            vs = v_ref[pl.ds(start, SUB), hh * LANES:(hh + 1) * LANES]
            z = _nt_dot(q_heads[hh][r0:], ks)
            sp = _softplus(z)
            if first_row is not None:
                sp = mask_top(sp)
            hi = sp.astype(BF16)
            lo = (sp - hi.astype(F32)).astype(BF16)
            g = _bdot(hi, tri) + _bdot(lo, tri)
            a = jnp.exp(z - g - later[r0:])
            if first_row is not None:
                a = mask_top(a)
            new.append((add_rows(later, g[:, 0:1]), add_rows(acc, _bdot(a.astype(BF16), vs))))
        return tuple(new)

    carry = tuple((jnp.zeros((bq, 1), F32), jnp.zeros((bq, LANES), F32)) for _ in range(2))
    for u in reversed(range(n_sub)):
        carry = sub_step(pl.multiple_of(i * bq + u * SUB, SUB), carry, u * SUB)

    def decayed(c):
        return (jnp.min(jnp.minimum(c[0][0], c[1][0])) >= EXP_UNDERFLOW).astype(jnp.int32)

    def back(state):
        n, _, c = state
        c = sub_step(pl.multiple_of(i * bq - (n + 1) * SUB, SUB), c, None)
        return n + 1, decayed(c), c

    _, _, carry = lax.while_loop(lambda st: (st[0] < i * n_sub) & (st[1] == 0), back,
                                 (jnp.int32(0), decayed(carry), carry))
    o_ref[...] = _pair_out(carry[0][1], carry[1][1])


def _attention_call(body, name, q, k, v, extra, seq, q_lanes, k_lanes, scratch=()):
    t = q.shape[0]
    nb, nq, npair = t // seq, seq // BQ, N_HEADS // 2
    q_spec = lambda w: pl.BlockSpec((BQ, w), lambda b, p, i: (b * nq + i, p))
    kv_spec = lambda w: pl.BlockSpec((seq, w), lambda b, p, i: (b, p))
    return pl.pallas_call(
        body,
        grid=(nb, npair, nq),
        in_specs=[q_spec(q_lanes), kv_spec(k_lanes), kv_spec(2 * LANES)]
        + [pl.BlockSpec(a.shape, lambda b, p, i: (0,) * a.ndim) for a in extra],
        out_specs=q_spec(LANES),
        out_shape=jax.ShapeDtypeStruct((t, npair * LANES), F32),
        scratch_shapes=list(scratch),
        compiler_params=_cparams(("arbitrary", "arbitrary", "arbitrary")),
        name=name,
    )(q, k, v, *extra)


def _rms(y, g):
    return y * lax.rsqrt(jnp.mean(y * y, axis=-1, keepdims=True) + NORM_EPS) * g


def _as_u32(words):
    return words if words.dtype == jnp.uint32 else pltpu.bitcast(words, jnp.uint32)


def _pack_bf16_pairs(x):
    n = x.shape[1] // 2
    return _as_u32(pltpu.pack_elementwise([x[:, :n], x[:, n:]], packed_dtype=BF16))


def _unpack_bf16_pairs(p):
    halves = [pltpu.unpack_elementwise(p, index=n, packed_dtype=BF16, unpacked_dtype=F32) for n in range(2)]
    return jnp.concatenate(halves, axis=1)


def _outproj_kernel(x_ref, ya_ref, yb_ref, ga_ref, gb_ref, wo_ref, gf_ref, wr_hi_ref, wr_lo_ref, br_ref,
                    ltri_ref, x2_ref, h2p_ref, meta_ref, cnt_ref, carry_ref):
    i = pl.program_id(0)

    @pl.when(i == 0)
    def _():
        carry_ref[...] = jnp.zeros_like(carry_ref)

    ya = _rms(ya_ref[...], ga_ref[...]).astype(BF16)
    yb = _rms(yb_ref[...], gb_ref[...]).astype(BF16)
    wa = ya_ref.shape[1]
    x2 = x_ref[...] + _bdot(ya, wo_ref[0:wa, :]) + _bdot(yb, wo_ref[wa:, :])
    x2_ref[...] = x2
    h2 = _rms(x2, gf_ref[...])
    h2p_ref[...] = _pack_bf16_pairs(h2)

    h2_hi, h2_lo = _split_bf16(h2, 2)
    logits = _dot3(h2_hi, h2_lo, wr_hi_ref, wr_lo_ref) + br_ref[...]
    tm = logits.shape[0]
    lane = lax.broadcasted_iota(jnp.int32, (tm, LANES), 1)
    lane_f = lane.astype(F32)
    work = logits
    vals, sels, idxs = [], [], []
    for _ in range(TOP_K):
        mx = jnp.max(work, axis=1, keepdims=True)
        idx = jnp.min(jnp.where(work == mx, lane_f, float(LANES)), axis=1, keepdims=True)
        sel = lane_f == idx
        vals.append(mx)
        idxs.append(idx)
        sels.append(sel)
        work = jnp.where(sel, -jnp.inf, work)
    exps = [jnp.exp(v - vals[0]) for v in vals]
    denom = exps[0] + exps[1] + exps[2] + exps[3]
    gates = [e / denom for e in exps]

    onehot = jnp.zeros((tm, LANES), F32)
    for sel in sels:
        onehot = onehot + jnp.where(sel, 1.0, 0.0)
    before = _bdot(ltri_ref[...], onehot.astype(BF16)) + carry_ref[0:1, :]
    meta = jnp.zeros((tm, LANES), F32)
    for k in range(TOP_K):
        rank = jnp.sum(jnp.where(sels[k], before, 0.0), axis=1, keepdims=True)
        meta = jnp.where(lane == k, idxs[k], meta)
        meta = jnp.where(lane == TOP_K + k, rank, meta)
        meta = jnp.where(lane == 2 * TOP_K + k, gates[k], meta)
    meta_ref[...] = meta
    total = carry_ref[0:1, :] + jnp.sum(onehot, axis=0, keepdims=True)
    carry_ref[...] = jnp.broadcast_to(total, carry_ref.shape)
    cnt_ref[...] = jnp.broadcast_to(total, cnt_ref.shape)


def _outproj(x2d, ya, yb, ga, gb, wo, gf, wr, br):
    t, d = x2d.shape
    tm = TM_ROUTE
    wa = ya.shape[1]
    ltri = (jnp.arange(tm)[:, None] > jnp.arange(tm)[None, :]).astype(BF16)
    wr_hi = wr.astype(BF16)
    wr_lo = (wr - wr_hi.astype(F32)).astype(BF16)
    tile = lambda w: pl.BlockSpec((tm, w), lambda i: (i, 0))
    whole = lambda a: pl.BlockSpec(a.shape, lambda i: (0, 0))
    return pl.pallas_call(
        _outproj_kernel,
        grid=(t // tm,),
        in_specs=[tile(d), tile(wa), tile(wa), whole(ga), whole(gb), whole(wo), whole(gf),
                  whole(wr_hi), whole(wr_lo), whole(br), whole(ltri)],
        out_specs=[tile(d), tile(d // 2), tile(LANES), pl.BlockSpec((8, LANES), lambda i: (0, 0))],
        out_shape=[jax.ShapeDtypeStruct((t, d), F32), jax.ShapeDtypeStruct((t, d // 2), jnp.uint32),
                   jax.ShapeDtypeStruct((t, LANES), F32), jax.ShapeDtypeStruct((8, LANES), F32)],
        scratch_shapes=[pltpu.VMEM((8, LANES), F32)],
        compiler_params=_cparams(("arbitrary",)),
        name="outproj",
    )(x2d, ya, yb, ga, gb, wo, gf, wr_hi, wr_lo, br, ltri)


def _rowmap_kernel(fill_lo_ref, fill_hi_ref, dest_ref, map_ref, *, n_slots, bm):
    n = dest_ref.shape[0]
    step = pl.program_id(0)

    @pl.when(step == 0)
    def _():
        def fill_range(e, _):
            def fill(j, _):
                map_ref[j] = n_slots + (j & (bm - 1))
                return 0
            return lax.fori_loop(fill_lo_ref[e], fill_hi_ref[e], fill, 0)
        lax.fori_loop(0, fill_lo_ref.shape[0], fill_range, 0)

    def body(j, _):
        map_ref[dest_ref[j]] = step * n + j
        return 0

    lax.fori_loop(0, n, body, 0, unroll=8)


def _rowmap(dest_flat, fill_lo, fill_hi, n_rows):
    n = dest_flat.shape[0]
    chunk = ROWMAP_CHUNK
    assert BM_MOE & (BM_MOE - 1) == 0
    grid_spec = pltpu.PrefetchScalarGridSpec(
        num_scalar_prefetch=2,
        grid=(n // chunk,),
        in_specs=[pl.BlockSpec((chunk,), lambda i, *_: (i,), memory_space=pltpu.SMEM)],
        out_specs=pl.BlockSpec((n_rows,), lambda i, *_: (0,), memory_space=pltpu.SMEM),
    )
    return pl.pallas_call(
        functools.partial(_rowmap_kernel, n_slots=n, bm=BM_MOE),
        grid_spec=grid_spec,
        out_shape=jax.ShapeDtypeStruct((n_rows,), jnp.int32),
        compiler_params=_cparams(("arbitrary",)),
        name="rowmap",
    )(fill_lo, fill_hi, dest_flat)


def _moe_kernel(be_ref, first_ref, nv_ref, wslot_ref, nexte_ref, sprev_ref, snext_ref, h_ref, wgu_ref, wd_ref,
                bgu_ref, bd_ref, ysg_ref, wgu_bf_ref, wd2_ref, wgu_f32_ref, wd_f32_ref, x0_ref, x1_ref,
                y0_ref, y1_ref, gsem, ssem, wsem):
    i = pl.program_id(0)
    nv = nv_ref[0]
    bm = x0_ref.shape[0]
    n_tokens = h_ref.shape[0]
    xbufs, ybufs = (x0_ref, x1_ref), (y0_ref, y1_ref)

    def gather_row(slots_ref, r, buf):
        token = slots_ref[r] & (n_tokens - 1)
        return pltpu.make_async_copy(h_ref.at[pl.ds(token, 1), :], xbufs[buf].at[pl.ds(r, 1), :], gsem.at[buf])

    def scatter_row(slots_ref, r, buf):
        return pltpu.make_async_copy(ybufs[buf].at[pl.ds(r, 1), :], ysg_ref.at[pl.ds(slots_ref[r], 1), :],
                                     ssem.at[buf])

    @pl.when(i == 0)
    def _():
        y1_ref[...] = jnp.zeros_like(y1_ref)
        spare = pltpu.make_async_copy(y1_ref, ysg_ref.at[pl.ds(ysg_ref.shape[0] - bm, bm), :], ssem.at[1])
        spare.start()
        spare.wait()
        for r in range(bm):
            gather_row(sprev_ref, r, 0).start()

    def weight_copies(expert, slot):
        return (pltpu.make_async_copy(wgu_ref.at[expert], wgu_f32_ref.at[slot], wsem.at[slot]),
                pltpu.make_async_copy(wd_ref.at[expert], wd_f32_ref.at[slot], wsem.at[slot]))

    @pl.when(i == 0)
    def _():
        for copy in weight_copies(be_ref[0], 0):
            copy.start()

    @pl.when(first_ref[i] == 1)
    def _():
        slot = wslot_ref[i]
        for copy in weight_copies(be_ref[i], slot):
            copy.wait()
        wgu_bf_ref[...] = wgu_f32_ref[slot].astype(BF16)
        wd = wd_f32_ref[slot]
        wd2_ref[...] = _as_u32(pltpu.pack_elementwise([wd, wd], packed_dtype=BF16))

        @pl.when(nexte_ref[i] >= 0)
        def _():
            for copy in weight_copies(nexte_ref[i], 1 - slot):
                copy.start()

    def compute(cur):
        nxt = 1 - cur
        for r in range(bm):
            gather_row(snext_ref, r, cur).wait()
        xb = _unpack_bf16_pairs(xbufs[cur][...]).astype(BF16)
        for r in range(bm):
            scatter_row(sprev_ref, r, nxt).start()
            gather_row(snext_ref, r, nxt).start()
        gu = _bdot(xb, wgu_bf_ref[...]) + bgu_ref[0]
        even = lax.broadcasted_iota(jnp.int32, (gu.shape[0], LANES), 1) % 2 == 0
        acts = []
        for c in range(gu.shape[1] // LANES):
            blk = gu[:, c * LANES:(c + 1) * LANES]
            gate = jnp.minimum(blk, SWIGLU_LIMIT)
            glu = gate * (1.0 / (1.0 + jnp.exp(-SWIGLU_ALPHA * gate)))
            up1 = jnp.clip(blk, -SWIGLU_LIMIT, SWIGLU_LIMIT) + 1.0
            acts.append(jnp.where(even, glu * pltpu.roll(up1, LANES - 1, axis=1), 0.0).astype(BF16))
        act = jnp.concatenate(acts, axis=1)
        wd2 = pltpu.bitcast(wd2_ref[...], BF16)
        ybufs[cur][...] = _pack_bf16_pairs(_bdot(act, wd2) + bd_ref[0])

    def drain(cur):
        nxt = 1 - cur
        for r in range(bm):
            scatter_row(sprev_ref, r, nxt).start()
        for r in range(bm):
            scatter_row(sprev_ref, r, nxt).wait()
        for r in range(bm):
            gather_row(snext_ref, r, cur).wait()

    for cur in range(2):
        parity = (i % 2) == cur

        @pl.when(parity & (i >= 1) & (i <= nv))
        def _():
            for r in range(bm):
                scatter_row(sprev_ref, r, cur).wait()

        pl.when(parity & (i < nv))(functools.partial(compute, cur))
        pl.when(parity & (i == nv))(functools.partial(drain, cur))


def _moe(block_e, first, n_valid, rowmap, h2p, wgu, wd, bgu, bd):
    t, half = h2p.shape
    assert t & (t - 1) == 0, "token count must be a power of two (slot -> token uses a bit mask)"
    bm = BM_MOE
    n_blocks = rowmap.shape[0] // bm
    dff, d = wd.shape[1:]
    n_steps = block_e.shape[0]
    wslot = ((jnp.cumsum(first) - 1) % 2).astype(jnp.int32)
    step = jnp.arange(n_steps, dtype=jnp.int32)
    later_first = jnp.where((first[None, :] == 1) & (step[None, :] > step[:, None]), step[None, :], n_steps)
    next_first = jnp.min(later_first, axis=1)
    next_e = jnp.where(next_first < n_steps, block_e[jnp.minimum(next_first, n_steps - 1)], -1).astype(jnp.int32)

    b_spec = lambda a: pl.BlockSpec((1,) + a.shape[1:], lambda i, be, *_: (be[i], 0, 0))
    slots = lambda index: pl.BlockSpec((bm,), index, memory_space=pltpu.SMEM)
    tile = pltpu.VMEM((bm, half), jnp.uint32)
    grid_spec = pltpu.PrefetchScalarGridSpec(
        num_scalar_prefetch=5,
        grid=(n_blocks + 1,),
        in_specs=[slots(lambda i, *_: (jnp.maximum(i - 1, 0),)),
                  slots(lambda i, *_: (jnp.minimum(i + 1, n_blocks - 1),)),
                  pl.BlockSpec(memory_space=pl.ANY), pl.BlockSpec(memory_space=pl.ANY),
                  pl.BlockSpec(memory_space=pl.ANY), b_spec(bgu), b_spec(bd)],
        out_specs=pl.BlockSpec(memory_space=pl.ANY),
        scratch_shapes=[pltpu.VMEM((d, 2 * dff), BF16), pltpu.VMEM((dff, d), jnp.uint32),
                        pltpu.VMEM((2, d, 2 * dff), F32), pltpu.VMEM((2, dff, d), F32),
                        tile, tile, tile, tile,
                        pltpu.SemaphoreType.DMA((2,)), pltpu.SemaphoreType.DMA((2,)),
                        pltpu.SemaphoreType.DMA((2,))],
    )
    return pl.pallas_call(
        _moe_kernel,
        grid_spec=grid_spec,
        out_shape=jax.ShapeDtypeStruct((TOP_K * t + bm, half), jnp.uint32),
        compiler_params=_cparams(("arbitrary",)),
        name="moe",
    )(block_e, first, n_valid, wslot, next_e, rowmap, rowmap, h2p, wgu, wd, bgu, bd)


def _combine_kernel(x2_ref, meta_ref, g_ref, *rest):
    ys_refs, o_ref = rest[:TOP_K], rest[TOP_K]
    meta = meta_ref[...]
    y = x2_ref[...]
    for k in range(TOP_K):
        y = y + meta[:, 2 * TOP_K + k:2 * TOP_K + k + 1] * _unpack_bf16_pairs(ys_refs[k][...])
    o_ref[...] = _rms(y, g_ref[...])


def _combine(x2, meta, g, ysg):
    t, d = x2.shape
    tm = TM_COMBINE
    tiles = t // tm
    slot_spec = lambda k: pl.BlockSpec((tm, ysg.shape[1]), lambda i: (k * tiles + i, 0))
    return pl.pallas_call(
        _combine_kernel,
        grid=(tiles,),
        in_specs=[pl.BlockSpec((tm, d), lambda i: (i, 0)),
                  pl.BlockSpec((tm, LANES), lambda i: (i, 0)),
                  pl.BlockSpec((1, d), lambda i: (0, 0))] + [slot_spec(k) for k in range(TOP_K)],
        out_specs=pl.BlockSpec((tm, d), lambda i: (i, 0)),
        out_shape=jax.ShapeDtypeStruct((t, d), F32),
        compiler_params=_cparams(("arbitrary",)),
        name="combine",
    )(x2, meta, g, *([ysg] * TOP_K))


def kernel(x, attn_norm_g, w_in, b_forget, fox_out_g, sb_out_g, w_out, ffn_norm_g, w_router, b_router,
           w_gate_up, b_gate_up, w_down, b_down, final_norm_g):
    b, s, d = x.shape
    t = b * s
    fw = N_HEADS * HEAD_DIM
    x2d = x.reshape(t, d)

    o = 0
    parts = []
    for width in (fw, fw, fw, N_HEADS, fw, fw, fw):
        parts.append(w_in[:, o:o + width])
        o += width
    wqa, wka, wva, wf, wqb, wkb, wvb = parts
    scale = 1.0 / math.sqrt(HEAD_DIM)
    w_all = jnp.concatenate(
        [_pad_heads(wqa * (scale * LOG2E)), _pad_heads(wka), _pad_heads(wva),
         wqb * scale, wkb, _pad_heads(wvb)], axis=1).astype(BF16)
    wf_pad = jnp.zeros((d, LANES), F32).at[:, :N_HEADS].set(wf)
    bf_pad = jnp.zeros((1, LANES), F32).at[0, :N_HEADS].set(b_forget)

    qa, ka, va, qb, kb, vb = _inproj(x2d, attn_norm_g.reshape(1, d), w_all, wf_pad, bf_pad, s)
    ya = _attention_call(_fox_kernel, "fox", qa, ka, va, (), s, 2 * LANES, 2 * LANES,
                         scratch=(pltpu.VMEM((8, LANES), F32), pltpu.VMEM((2, BQ, BK), F32)))
    tri = (jnp.arange(SUB)[:, None] >= jnp.arange(SUB)[None, :]).astype(BF16)
    yb = _attention_call(_sb_kernel, "sb", qb, kb, vb, (tri,), s, LANES, LANES)

    wr = jnp.zeros((d, LANES), F32).at[:, :N_EXPERTS].set(w_router)
    br = jnp.full((1, LANES), NEG_BIG, F32).at[0, :N_EXPERTS].set(b_router)
    x2, h2p, meta, cnt = _outproj(x2d, ya, yb, fox_out_g.reshape(1, fw), sb_out_g.reshape(1, fw),
                                 w_out.astype(BF16), ffn_norm_g.reshape(1, d), wr, br)

    bm = BM_MOE
    n_blocks = (t * TOP_K + N_EXPERTS * (bm - 1)) // bm
    counts = cnt[0, :N_EXPERTS].astype(jnp.int32)
    padded = (counts + bm - 1) // bm * bm
    ends = jnp.cumsum(padded)
    starts = ends - padded
    top_idx = meta[:, 0:TOP_K].astype(jnp.int32)
    rank = meta[:, TOP_K:2 * TOP_K].astype(jnp.int32)
    dest = (starts[top_idx] + rank).T.reshape(t * TOP_K)
    block_start = jnp.arange(n_blocks + 1, dtype=jnp.int32) * bm
    block_e = jnp.minimum(jnp.sum(block_start[:, None] >= ends[None, :], axis=1), N_EXPERTS - 1).astype(jnp.int32)
    first = jnp.concatenate([jnp.ones((1,), jnp.int32), (block_e[1:] != block_e[:-1]).astype(jnp.int32)])
    n_valid = (ends[-1:] // bm).astype(jnp.int32)

    n_rows = n_blocks * bm
    fill_lo = jnp.concatenate([starts + counts, ends[-1:]]).astype(jnp.int32)
    fill_hi = jnp.concatenate([ends, jnp.full((1,), n_rows, jnp.int32)]).astype(jnp.int32)
    rowmap = _rowmap(dest, fill_lo, fill_hi, n_rows)
    dff = w_down.shape[1]
    ysg = _moe(block_e, first, n_valid, rowmap, h2p, w_gate_up, w_down,
               b_gate_up.reshape(N_EXPERTS, 1, 2 * dff), b_down.reshape(N_EXPERTS, 1, d))
    out = _combine(x2, meta, final_norm_g.reshape(1, d), ysg)
    return out.reshape(b, s, d)
```

```python
import functools
import math

import jax
import jax.numpy as jnp
from jax import lax
from jax.experimental import pallas as pl
from jax.experimental.pallas import tpu as pltpu

F32 = jnp.float32
BF16 = jnp.bfloat16

HEAD_DIM = 64
N_HEADS = 8
N_EXPERTS = 32
TOP_K = 4
LANES = 128
NORM_EPS = 1e-5
SWIGLU_LIMIT = 7.0
SWIGLU_ALPHA = 1.702
LOG2E = math.log2(math.e)
NEG_BIG = -1e30
N_BIAS_PARTS = 3
EXP_UNDERFLOW = 110.0
EXP2_UNDERFLOW = 152.0

TM_PROJ = 512
BQ = 512
BK = 512
SUB = 256
TM_ROUTE = 512
ROWMAP_CHUNK = 4096
TM_COMBINE = 256
BM_MOE = 256
VMEM_LIMIT = 56 * 1024 * 1024


def _cparams(sem):
    return pltpu.CompilerParams(dimension_semantics=sem, vmem_limit_bytes=VMEM_LIMIT)


def _bdot(a, b):
    return jnp.dot(a, b, preferred_element_type=F32)


def _split_bf16(x, parts):
    out = []
    for _ in range(parts):
        piece = x.astype(BF16)
        out.append(piece)
        x = x - piece.astype(F32)
    return out


def _dot3(a_hi, a_lo, b_hi_ref, b_lo_ref):
    return _bdot(a_hi, b_hi_ref[...]) + _bdot(a_lo, b_hi_ref[...]) + _bdot(a_hi, b_lo_ref[...])


def _nt_dot(a, b):
    return lax.dot_general(a, b, (((1,), (1,)), ((), ())), preferred_element_type=F32)


def _inproj_kernel(x_ref, g_ref, w_ref, wf_hi_ref, wf_lo_ref, bf_ref, tri_ref, place_ref,
                   qa_ref, ka_ref, va_ref, qb_ref, kb_ref, vb_ref, carry_ref, *, tiles_per_seq):
    i = pl.program_id(0)

    @pl.when(i % tiles_per_seq == 0)
    def _():
        carry_ref[...] = jnp.zeros_like(carry_ref)

    x = x_ref[...]
    ms = jnp.mean(x * x, axis=-1, keepdims=True)
    h = x * lax.rsqrt(ms + NORM_EPS) * g_ref[...]
    hb, h_lo = _split_bf16(h, 2)

    logit = _dot3(hb, h_lo, wf_hi_ref, wf_lo_ref) + bf_ref[...]
    logf = jnp.minimum(logit, 0.0) - jnp.log(1.0 + jnp.exp(-jnp.abs(logit)))
    c = carry_ref[0:1, :]
    for piece in _split_bf16(logf, 3):
        c = c + _bdot(tri_ref[...], piece)
    tm = c.shape[0]
    carry_ref[...] = jnp.broadcast_to(c[tm - 1:tm, :], carry_ref.shape)
    placed = jnp.zeros((tm, ka_ref.shape[1]), F32)
    for n, part in enumerate(_split_bf16(-LOG2E * c, N_BIAS_PARTS)):
        placed = placed + _bdot(part, place_ref[n])

    lane = lax.broadcasted_iota(jnp.int32, (1, qa_ref.shape[1]), 1) % LANES
    q_extra = jnp.where((lane >= HEAD_DIM) & (lane < HEAD_DIM + N_BIAS_PARTS), 1.0, 0.0)
    v_extra = jnp.where(lane >= HEAD_DIM, 1.0, 0.0)
    extras = (q_extra, placed, v_extra, None, None, None)
    offset = 0
    for ref, extra in zip((qa_ref, ka_ref, va_ref, qb_ref, kb_ref, vb_ref), extras):
        width = ref.shape[1]
        out = _bdot(hb, w_ref[:, offset:offset + width])
        ref[...] = (out if extra is None else out + extra).astype(BF16)
        offset += width


def _pad_heads(w):
    d = w.shape[0]
    w = w.reshape(d, N_HEADS, HEAD_DIM)
    return jnp.concatenate([w, jnp.zeros_like(w)], axis=2).reshape(d, N_HEADS * LANES)


def _inproj(x2d, g, w_all, wf, bf, seq):
    t, d = x2d.shape
    tm = TM_PROJ
    width = N_HEADS * LANES
    tiles_per_seq = seq // tm
    tri = (jnp.arange(tm)[:, None] >= jnp.arange(tm)[None, :]).astype(BF16)
    wf_hi = wf.astype(BF16)
    wf_lo = (wf - wf_hi.astype(F32)).astype(BF16)
    src = jnp.arange(LANES)[:, None]
    dst = jnp.arange(width)[None, :]
    place = jnp.stack([((dst == src * LANES + HEAD_DIM + n) & (src < N_HEADS)).astype(BF16)
                       for n in range(N_BIAS_PARTS)])
    pair_width = N_HEADS * HEAD_DIM
    widths = (width, width, width, pair_width, pair_width, width)
    assert sum(widths) == w_all.shape[1]
    whole = lambda a: pl.BlockSpec(a.shape, lambda i: (0,) * a.ndim)
    return pl.pallas_call(
        functools.partial(_inproj_kernel, tiles_per_seq=tiles_per_seq),
        grid=(t // tm,),
        in_specs=[pl.BlockSpec((tm, d), lambda i: (i, 0)), whole(g), whole(w_all), whole(wf_hi), whole(wf_lo),
                  whole(bf), whole(tri), whole(place)],
        out_specs=[pl.BlockSpec((tm, w), lambda i: (i, 0)) for w in widths],
        out_shape=[jax.ShapeDtypeStruct((t, w), BF16) for w in widths],
        scratch_shapes=[pltpu.VMEM((8, LANES), F32)],
        compiler_params=_cparams(("arbitrary",)),
        name="inproj",
    )(x2d, g, w_all, wf_hi, wf_lo, bf, tri, place)


def _pair_out(o_even, o_odd):
    lane = lax.broadcasted_iota(jnp.int32, o_even.shape, 1)
    return jnp.where(lane < HEAD_DIM, o_even, pltpu.roll(o_odd, HEAD_DIM, axis=1))


def _head_sq_norm(tile):
    tile = tile.astype(F32)
    lane = lax.broadcasted_iota(jnp.int32, tile.shape, 1)
    return jnp.sum(jnp.where(lane < HEAD_DIM, tile * tile, 0.0), axis=1, keepdims=True)


def _fox_kernel(q_ref, k_ref, v_ref, o_ref, kmax_ref, s_ref):
    i = pl.program_id(2)
    bq = q_ref.shape[0]
    n_seq_blocks = k_ref.shape[0] // BK

    @pl.when(i == 0)
    def _():
        def body(n, best):
            start = pl.multiple_of(n * BK, BK)
            return tuple(
                jnp.maximum(best[hh], jnp.max(_head_sq_norm(k_ref[pl.ds(start, BK), hh * LANES:(hh + 1) * LANES]),
                                              axis=0, keepdims=True))
                for hh in range(2))
        best = lax.fori_loop(0, n_seq_blocks, body, (jnp.zeros((1, 1), F32), jnp.zeros((1, 1), F32)))
        for hh in range(2):
            kmax_ref[hh:hh + 1, :] = jnp.broadcast_to(jnp.sqrt(best[hh]), (1, LANES))

    q_heads = (q_ref[:, 0:LANES], q_ref[:, LANES:2 * LANES])
    reach = tuple(jnp.sqrt(_head_sq_norm(q_heads[hh])) * kmax_ref[hh:hh + 1, 0:1] for hh in range(2))
    row = lax.broadcasted_iota(jnp.int32, (bq, BK), 0)
    col = lax.broadcasted_iota(jnp.int32, (bq, BK), 1)
    causal = col <= row
    lane1 = lax.broadcasted_iota(jnp.int32, (1, LANES), 1)
    bias_lanes = (lane1 >= HEAD_DIM) & (lane1 < HEAD_DIM + N_BIAS_PARTS)

    def scores(j, hh):
        start = pl.multiple_of(j * BK, BK)
        return _nt_dot(q_heads[hh], k_ref[pl.ds(start, BK), hh * LANES:(hh + 1) * LANES])

    def look_ahead(j):
        nxt = jnp.maximum(j - 1, 0)
        return [scores(nxt, hh) for hh in range(2)]

    def step(j, carry, s_pair):
        start = pl.multiple_of(j * BK, BK)
        new = []
        slack = None
        for hh in range(2):
            m, acc = carry[hh]
            s = s_pair[hh]
            vs = v_ref[pl.ds(start, BK), hh * LANES:(hh + 1) * LANES]
            m_new = jnp.maximum(m, jnp.max(s, axis=1, keepdims=True))
            p = jnp.exp2(s - m_new)
            acc = jnp.exp2(m - m_new) * acc + _bdot(p.astype(BF16), vs)
            new.append((m_new, acc))
            first = k_ref[pl.ds(start, 1), hh * LANES:(hh + 1) * LANES].astype(F32)
            bias0 = jnp.sum(jnp.where(bias_lanes, first, 0.0), axis=1, keepdims=True)
            bound = jnp.max(reach[hh] + bias0 - m_new)
            slack = bound if slack is None else jnp.maximum(slack, bound)
        return tuple(new), (slack < -EXP2_UNDERFLOW).astype(jnp.int32)

    init = tuple((jnp.full((bq, 1), NEG_BIG, F32), jnp.zeros((bq, LANES), F32)) for _ in range(2))
    ahead = look_ahead(i)
    carry, done = step(i, init, [jnp.where(causal, scores(i, hh), NEG_BIG) for hh in range(2)])
    for hh in range(2):
        s_ref[hh] = ahead[hh]

    def block(state):
        n, _, c = state
        j = i - 1 - n
        s_pair = [s_ref[hh] for hh in range(2)]
        ahead = look_ahead(j)
        c, done = step(j, c, s_pair)
        for hh in range(2):
            s_ref[hh] = ahead[hh]
        return n + 1, done, c

    _, _, carry = lax.while_loop(lambda st: (st[0] < i) & (st[1] == 0), block, (jnp.int32(0), done, carry))
    (_, acc_a), (_, acc_b) = carry
    norm = lambda acc: acc / pltpu.roll(acc, HEAD_DIM, axis=1)
    o_ref[...] = _pair_out(norm(acc_a), norm(acc_b))


def _softplus(z):
    return jnp.maximum(z, 0.0) + jnp.log(1.0 + jnp.exp(-jnp.abs(z)))


def _sb_kernel(q_ref, k_ref, v_ref, tri_ref, o_ref):
    i = pl.program_id(2)
    bq = q_ref.shape[0]
    n_sub = bq // SUB
    q2 = q_ref[...]
    lane_q = lax.broadcasted_iota(jnp.int32, q2.shape, 1)
    q_heads = (jnp.where(lane_q < HEAD_DIM, q2, jnp.zeros_like(q2)),
               jnp.where(lane_q >= HEAD_DIM, q2, jnp.zeros_like(q2)))
    row = lax.broadcasted_iota(jnp.int32, (SUB, SUB), 0)
    col = lax.broadcasted_iota(jnp.int32, (SUB, SUB), 1)
    strict = col < row
    tri = tri_ref[...]

    def sub_step(start, carry, first_row):
        r0 = 0 if first_row is None else first_row

        def mask_top(x):
            top = jnp.where(strict, x[:SUB], 0.0)
            return top if x.shape[0] == SUB else jnp.concatenate([top, x[SUB:]], axis=0)

        def add_rows(full, part):
            return full + part if r0 == 0 else jnp.concatenate([full[:r0], full[r0:] + part], axis=0)

        ks = k_ref[pl.ds(start, SUB), :]
        new = []
        for hh in range(2):
            later, acc = carry[hh]
            vs = v_ref[pl.ds(start, SUB), hh * LANES:(hh + 1) * LANES]
            z = _nt_dot(q_heads[hh][r0:], ks)
            sp = _softplus(z)
            if first_row is not None:
                sp = mask_top(sp)
            hi = sp.astype(BF16)
            lo = (sp - hi.astype(F32)).astype(BF16)
            g = _bdot(hi, tri) + _bdot(lo, tri)
            a = jnp.exp(z - g - later[r0:])
            if first_row is not None:
                a = mask_top(a)
            new.append((add_rows(later, g[:, 0:1]), add_rows(acc, _bdot(a.astype(BF16), vs))))
        return tuple(new)

    carry = tuple((jnp.zeros((bq, 1), F32), jnp.zeros((bq, LANES), F32)) for _ in range(2))
    for u in reversed(range(n_sub)):
        carry = sub_step(pl.multiple_of(i * bq + u * SUB, SUB), carry, u * SUB)

    def decayed(c):
        return (jnp.min(jnp.minimum(c[0][0], c[1][0])) >= EXP_UNDERFLOW).astype(jnp.int32)

    def back(state):
        n, _, c = state
        c = sub_step(pl.multiple_of(i * bq - (n + 1) * SUB, SUB), c, None)
        return n + 1, decayed(c), c

    _, _, carry = lax.while_loop(lambda st: (st[0] < i * n_sub) & (st[1] == 0), back,
                                 (jnp.int32(0), decayed(carry), carry))
    o_ref[...] = _pair_out(carry[0][1], carry[1][1])


def _attention_call(body, name, q, k, v, extra, seq, q_lanes, k_lanes, scratch=()):
    t = q.shape[0]
    nb, nq, npair = t // seq, seq // BQ, N_HEADS // 2
    q_spec = lambda w: pl.BlockSpec((BQ, w), lambda b, p, i: (b * nq + i, p))
    kv_spec = lambda w: pl.BlockSpec((seq, w), lambda b, p, i: (b, p))
    return pl.pallas_call(
        body,
        grid=(nb, npair, nq),
        in_specs=[q_spec(q_lanes), kv_spec(k_lanes), kv_spec(2 * LANES)]
        + [pl.BlockSpec(a.shape, lambda b, p, i: (0,) * a.ndim) for a in extra],
        out_specs=q_spec(LANES),
        out_shape=jax.ShapeDtypeStruct((t, npair * LANES), F32),
        scratch_shapes=list(scratch),
        compiler_params=_cparams(("arbitrary", "arbitrary", "arbitrary")),
        name=name,
    )(q, k, v, *extra)


def _rms(y, g):
    return y * lax.rsqrt(jnp.mean(y * y, axis=-1, keepdims=True) + NORM_EPS) * g


def _as_u32(words):
    return words if words.dtype == jnp.uint32 else pltpu.bitcast(words, jnp.uint32)


def _pack_bf16_pairs(x):
    n = x.shape[1] // 2
    return _as_u32(pltpu.pack_elementwise([x[:, :n], x[:, n:]], packed_dtype=BF16))


def _unpack_bf16_pairs(p):
    halves = [pltpu.unpack_elementwise(p, index=n, packed_dtype=BF16, unpacked_dtype=F32) for n in range(2)]
    return jnp.concatenate(halves, axis=1)


def _outproj_kernel(x_ref, ya_ref, yb_ref, ga_ref, gb_ref, wo_ref, gf_ref, wr_hi_ref, wr_lo_ref, br_ref,
                    ltri_ref, x2_ref, h2p_ref, meta_ref, cnt_ref, carry_ref):
    i = pl.program_id(0)

    @pl.when(i == 0)
    def _():
        carry_ref[...] = jnp.zeros_like(carry_ref)

    ya = _rms(ya_ref[...], ga_ref[...]).astype(BF16)
    yb = _rms(yb_ref[...], gb_ref[...]).astype(BF16)
    wa = ya_ref.shape[1]
    x2 = x_ref[...] + _bdot(ya, wo_ref[0:wa, :]) + _bdot(yb, wo_ref[wa:, :])
    x2_ref[...] = x2
    h2 = _rms(x2, gf_ref[...])
    h2p_ref[...] = _pack_bf16_pairs(h2)

    h2_hi, h2_lo = _split_bf16(h2, 2)
    logits = _dot3(h2_hi, h2_lo, wr_hi_ref, wr_lo_ref) + br_ref[...]
    tm = logits.shape[0]
    lane = lax.broadcasted_iota(jnp.int32, (tm, LANES), 1)
    lane_f = lane.astype(F32)
    work = logits
    vals, sels, idxs = [], [], []
    for _ in range(TOP_K):
        mx = jnp.max(work, axis=1, keepdims=True)
        idx = jnp.min(jnp.where(work == mx, lane_f, float(LANES)), axis=1, keepdims=True)
        sel = lane_f == idx
        vals.append(mx)
        idxs.append(idx)
        sels.append(sel)
        work = jnp.where(sel, -jnp.inf, work)
    exps = [jnp.exp(v - vals[0]) for v in vals]
    denom = exps[0] + exps[1] + exps[2] + exps[3]
    gates = [e / denom for e in exps]

    onehot = jnp.zeros((tm, LANES), F32)
    for sel in sels:
        onehot = onehot + jnp.where(sel, 1.0, 0.0)
    before = _bdot(ltri_ref[...], onehot.astype(BF16)) + carry_ref[0:1, :]
    meta = jnp.zeros((tm, LANES), F32)
    for k in range(TOP_K):
        rank = jnp.sum(jnp.where(sels[k], before, 0.0), axis=1, keepdims=True)
        meta = jnp.where(lane == k, idxs[k], meta)
        meta = jnp.where(lane == TOP_K + k, rank, meta)
        meta = jnp.where(lane == 2 * TOP_K + k, gates[k], meta)
    meta_ref[...] = meta
    total = carry_ref[0:1, :] + jnp.sum(onehot, axis=0, keepdims=True)
    carry_ref[...] = jnp.broadcast_to(total, carry_ref.shape)
    cnt_ref[...] = jnp.broadcast_to(total, cnt_ref.shape)


def _outproj(x2d, ya, yb, ga, gb, wo, gf, wr, br):
    t, d = x2d.shape
    tm = TM_ROUTE
    wa = ya.shape[1]
    ltri = (jnp.arange(tm)[:, None] > jnp.arange(tm)[None, :]).astype(BF16)
    wr_hi = wr.astype(BF16)
    wr_lo = (wr - wr_hi.astype(F32)).astype(BF16)
    tile = lambda w: pl.BlockSpec((tm, w), lambda i: (i, 0))
    whole = lambda a: pl.BlockSpec(a.shape, lambda i: (0, 0))
    return pl.pallas_call(
        _outproj_kernel,
        grid=(t // tm,),
        in_specs=[tile(d), tile(wa), tile(wa), whole(ga), whole(gb), whole(wo), whole(gf),
                  whole(wr_hi), whole(wr_lo), whole(br), whole(ltri)],
        out_specs=[tile(d), tile(d // 2), tile(LANES), pl.BlockSpec((8, LANES), lambda i: (0, 0))],
        out_shape=[jax.ShapeDtypeStruct((t, d), F32), jax.ShapeDtypeStruct((t, d // 2), jnp.uint32),
                   jax.ShapeDtypeStruct((t, LANES), F32), jax.ShapeDtypeStruct((8, LANES), F32)],
        scratch_shapes=[pltpu.VMEM((8, LANES), F32)],
        compiler_params=_cparams(("arbitrary",)),
        name="outproj",
    )(x2d, ya, yb, ga, gb, wo, gf, wr_hi, wr_lo, br, ltri)


def _rowmap_kernel(fill_lo_ref, fill_hi_ref, dest_ref, map_ref, *, n_slots, bm):
    n = dest_ref.shape[0]
    step = pl.program_id(0)

    @pl.when(step == 0)
    def _():
        def fill_range(e, _):
            def fill(j, _):
                map_ref[j] = n_slots + (j & (bm - 1))
                return 0
            return lax.fori_loop(fill_lo_ref[e], fill_hi_ref[e], fill, 0)
        lax.fori_loop(0, fill_lo_ref.shape[0], fill_range, 0)

    def body(j, _):
        map_ref[dest_ref[j]] = step * n + j
        return 0

    lax.fori_loop(0, n, body, 0, unroll=8)


def _rowmap(dest_flat, fill_lo, fill_hi, n_rows):
    n = dest_flat.shape[0]
    chunk = ROWMAP_CHUNK
    assert BM_MOE & (BM_MOE - 1) == 0
    grid_spec = pltpu.PrefetchScalarGridSpec(
        num_scalar_prefetch=2,
        grid=(n // chunk,),
        in_specs=[pl.BlockSpec((chunk,), lambda i, *_: (i,), memory_space=pltpu.SMEM)],
        out_specs=pl.BlockSpec((n_rows,), lambda i, *_: (0,), memory_space=pltpu.SMEM),
    )
    return pl.pallas_call(
        functools.partial(_rowmap_kernel, n_slots=n, bm=BM_MOE),
        grid_spec=grid_spec,
        out_shape=jax.ShapeDtypeStruct((n_rows,), jnp.int32),
        compiler_params=_cparams(("arbitrary",)),
        name="rowmap",
    )(fill_lo, fill_hi, dest_flat)


def _moe_kernel(be_ref, first_ref, nv_ref, wslot_ref, nexte_ref, sprev_ref, snext_ref, h_ref, wgu_ref, wd_ref,
                bgu_ref, bd_ref, ysg_ref, wgu_bf_ref, wd2_ref, wgu_f32_ref, wd_f32_ref, x0_ref, x1_ref,
                y0_ref, y1_ref, gsem, ssem, wsem):
    i = pl.program_id(0)
    nv = nv_ref[0]
    bm = x0_ref.shape[0]
    n_tokens = h_ref.shape[0]
    xbufs, ybufs = (x0_ref, x1_ref), (y0_ref, y1_ref)

    def gather_row(slots_ref, r, buf):
        token = slots_ref[r] & (n_tokens - 1)
        return pltpu.make_async_copy(h_ref.at[pl.ds(token, 1), :], xbufs[buf].at[pl.ds(r, 1), :], gsem.at[buf])

    def scatter_row(slots_ref, r, buf):
        return pltpu.make_async_copy(ybufs[buf].at[pl.ds(r, 1), :], ysg_ref.at[pl.ds(slots_ref[r], 1), :],
                                     ssem.at[buf])

    @pl.when(i == 0)
    def _():
        y1_ref[...] = jnp.zeros_like(y1_ref)
        spare = pltpu.make_async_copy(y1_ref, ysg_ref.at[pl.ds(ysg_ref.shape[0] - bm, bm), :], ssem.at[1])
        spare.start()
        spare.wait()
        for r in range(bm):
            gather_row(sprev_ref, r, 0).start()

    def weight_copies(expert, slot):
        return (pltpu.make_async_copy(wgu_ref.at[expert], wgu_f32_ref.at[slot], wsem.at[slot]),
                pltpu.make_async_copy(wd_ref.at[expert], wd_f32_ref.at[slot], wsem.at[slot]))

    @pl.when(i == 0)
    def _():
        for copy in weight_copies(be_ref[0], 0):
            copy.start(priority=1)

    @pl.when(first_ref[i] == 1)
    def _():
        slot = wslot_ref[i]
        for copy in weight_copies(be_ref[i], slot):
            copy.wait()
        wgu_bf_ref[...] = wgu_f32_ref[slot].astype(BF16)
        wd = wd_f32_ref[slot]
        wd2_ref[...] = _as_u32(pltpu.pack_elementwise([wd, wd], packed_dtype=BF16))

        @pl.when(nexte_ref[i] >= 0)
        def _():
            for copy in weight_copies(nexte_ref[i], 1 - slot):
                copy.start(priority=1)

    def compute(cur):
        nxt = 1 - cur
        for r in range(bm):
            gather_row(snext_ref, r, cur).wait()
        xb = _unpack_bf16_pairs(xbufs[cur][...]).astype(BF16)
        for r in range(bm):
            scatter_row(sprev_ref, r, nxt).start(priority=1)
            gather_row(snext_ref, r, nxt).start()
        gu = _bdot(xb, wgu_bf_ref[...]) + bgu_ref[0]
        even = lax.broadcasted_iota(jnp.int32, (gu.shape[0], LANES), 1) % 2 == 0
        acts = []
        for c in range(gu.shape[1] // LANES):
            blk = gu[:, c * LANES:(c + 1) * LANES]
            gate = jnp.minimum(blk, SWIGLU_LIMIT)
            glu = gate * (1.0 / (1.0 + jnp.exp(-SWIGLU_ALPHA * gate)))
            up1 = jnp.clip(blk, -SWIGLU_LIMIT, SWIGLU_LIMIT) + 1.0
            acts.append(jnp.where(even, glu * pltpu.roll(up1, LANES - 1, axis=1), 0.0).astype(BF16))
        act = jnp.concatenate(acts, axis=1)
        wd2 = pltpu.bitcast(wd2_ref[...], BF16)
        ybufs[cur][...] = _pack_bf16_pairs(_bdot(act, wd2) + bd_ref[0])

    def drain(cur):
        nxt = 1 - cur
        for r in range(bm):
            scatter_row(sprev_ref, r, nxt).start()
        for r in range(bm):
            scatter_row(sprev_ref, r, nxt).wait()
        for r in range(bm):
            gather_row(snext_ref, r, cur).wait()

    for cur in range(2):
        parity = (i % 2) == cur

        @pl.when(parity & (i >= 1) & (i <= nv))
        def _():
            for r in range(bm):
                scatter_row(sprev_ref, r, cur).wait()

        pl.when(parity & (i < nv))(functools.partial(compute, cur))
        pl.when(parity & (i == nv))(functools.partial(drain, cur))


def _moe(block_e, first, n_valid, rowmap, h2p, wgu, wd, bgu, bd):
    t, half = h2p.shape
    assert t & (t - 1) == 0, "token count must be a power of two (slot -> token uses a bit mask)"
    bm = BM_MOE
    n_blocks = rowmap.shape[0] // bm
    dff, d = wd.shape[1:]
    n_steps = block_e.shape[0]
    wslot = ((jnp.cumsum(first) - 1) % 2).astype(jnp.int32)
    step = jnp.arange(n_steps, dtype=jnp.int32)
    later_first = jnp.where((first[None, :] == 1) & (step[None, :] > step[:, None]), step[None, :], n_steps)
    next_first = jnp.min(later_first, axis=1)
    next_e = jnp.where(next_first < n_steps, block_e[jnp.minimum(next_first, n_steps - 1)], -1).astype(jnp.int32)

    b_spec = lambda a: pl.BlockSpec((1,) + a.shape[1:], lambda i, be, *_: (be[i], 0, 0))
    slots = lambda index: pl.BlockSpec((bm,), index, memory_space=pltpu.SMEM)
    tile = pltpu.VMEM((bm, half), jnp.uint32)
    grid_spec = pltpu.PrefetchScalarGridSpec(
        num_scalar_prefetch=5,
        grid=(n_blocks + 1,),
        in_specs=[slots(lambda i, *_: (jnp.maximum(i - 1, 0),)),
                  slots(lambda i, *_: (jnp.minimum(i + 1, n_blocks - 1),)),
                  pl.BlockSpec(memory_space=pl.ANY), pl.BlockSpec(memory_space=pl.ANY),
                  pl.BlockSpec(memory_space=pl.ANY), b_spec(bgu), b_spec(bd)],
        out_specs=pl.BlockSpec(memory_space=pl.ANY),
        scratch_shapes=[pltpu.VMEM((d, 2 * dff), BF16), pltpu.VMEM((dff, d), jnp.uint32),
                        pltpu.VMEM((2, d, 2 * dff), F32), pltpu.VMEM((2, dff, d), F32),
                        tile, tile, tile, tile,
                        pltpu.SemaphoreType.DMA((2,)), pltpu.SemaphoreType.DMA((2,)),
                        pltpu.SemaphoreType.DMA((2,))],
    )
    return pl.pallas_call(
        _moe_kernel,
        grid_spec=grid_spec,
        out_shape=jax.ShapeDtypeStruct((TOP_K * t + bm, half), jnp.uint32),
        compiler_params=_cparams(("arbitrary",)),
        name="moe",
    )(block_e, first, n_valid, wslot, next_e, rowmap, rowmap, h2p, wgu, wd, bgu, bd)


def _combine_kernel(x2_ref, meta_ref, g_ref, *rest):
    ys_refs, o_ref = rest[:TOP_K], rest[TOP_K]
    meta = meta_ref[...]
    y = x2_ref[...]
    for k in range(TOP_K):
        y = y + meta[:, 2 * TOP_K + k:2 * TOP_K + k + 1] * _unpack_bf16_pairs(ys_refs[k][...])
    o_ref[...] = _rms(y, g_ref[...])


def _combine(x2, meta, g, ysg):
    t, d = x2.shape
    tm = TM_COMBINE
    tiles = t // tm
    slot_spec = lambda k: pl.BlockSpec((tm, ysg.shape[1]), lambda i: (k * tiles + i, 0))
    return pl.pallas_call(
        _combine_kernel,
        grid=(tiles,),
        in_specs=[pl.BlockSpec((tm, d), lambda i: (i, 0)),
                  pl.BlockSpec((tm, LANES), lambda i: (i, 0)),
                  pl.BlockSpec((1, d), lambda i: (0, 0))] + [slot_spec(k) for k in range(TOP_K)],
        out_specs=pl.BlockSpec((tm, d), lambda i: (i, 0)),
        out_shape=jax.ShapeDtypeStruct((t, d), F32),
        compiler_params=_cparams(("arbitrary",)),
        name="combine",
    )(x2, meta, g, *([ysg] * TOP_K))


def kernel(x, attn_norm_g, w_in, b_forget, fox_out_g, sb_out_g, w_out, ffn_norm_g, w_router, b_router,
           w_gate_up, b_gate_up, w_down, b_down, final_norm_g):
    b, s, d = x.shape
    t = b * s
    fw = N_HEADS * HEAD_DIM
    x2d = x.reshape(t, d)

    o = 0
    parts = []
    for width in (fw, fw, fw, N_HEADS, fw, fw, fw):
        parts.append(w_in[:, o:o + width])
        o += width
    wqa, wka, wva, wf, wqb, wkb, wvb = parts
    scale = 1.0 / math.sqrt(HEAD_DIM)
    w_all = jnp.concatenate(
        [_pad_heads(wqa * (scale * LOG2E)), _pad_heads(wka), _pad_heads(wva),
         wqb * scale, wkb, _pad_heads(wvb)], axis=1).astype(BF16)
    wf_pad = jnp.zeros((d, LANES), F32).at[:, :N_HEADS].set(wf)
    bf_pad = jnp.zeros((1, LANES), F32).at[0, :N_HEADS].set(b_forget)

    qa, ka, va, qb, kb, vb = _inproj(x2d, attn_norm_g.reshape(1, d), w_all, wf_pad, bf_pad, s)
    ya = _attention_call(_fox_kernel, "fox", qa, ka, va, (), s, 2 * LANES, 2 * LANES,
                         scratch=(pltpu.VMEM((8, LANES), F32), pltpu.VMEM((2, BQ, BK), F32)))
    tri = (jnp.arange(SUB)[:, None] >= jnp.arange(SUB)[None, :]).astype(BF16)
    yb = _attention_call(_sb_kernel, "sb", qb, kb, vb, (tri,), s, LANES, LANES)

    wr = jnp.zeros((d, LANES), F32).at[:, :N_EXPERTS].set(w_router)
    br = jnp.full((1, LANES), NEG_BIG, F32).at[0, :N_EXPERTS].set(b_router)
    x2, h2p, meta, cnt = _outproj(x2d, ya, yb, fox_out_g.reshape(1, fw), sb_out_g.reshape(1, fw),
                                 w_out.astype(BF16), ffn_norm_g.reshape(1, d), wr, br)

    bm = BM_MOE
    n_blocks = (t * TOP_K + N_EXPERTS * (bm - 1)) // bm
    counts = cnt[0, :N_EXPERTS].astype(jnp.int32)
    padded = (counts + bm - 1) // bm * bm
    ends = jnp.cumsum(padded)
    starts = ends - padded
    top_idx = meta[:, 0:TOP_K].astype(jnp.int32)
    rank = meta[:, TOP_K:2 * TOP_K].astype(jnp.int32)
    dest = (starts[top_idx] + rank).T.reshape(t * TOP_K)
    block_start = jnp.arange(n_blocks + 1, dtype=jnp.int32) * bm
    block_e = jnp.minimum(jnp.sum(block_start[:, None] >= ends[None, :], axis=1), N_EXPERTS - 1).astype(jnp.int32)
    first = jnp.concatenate([jnp.ones((1,), jnp.int32), (block_e[1:] != block_e[:-1]).astype(jnp.int32)])
    n_valid = (ends[-1:] // bm).astype(jnp.int32)

    n_rows = n_blocks * bm
    fill_lo = jnp.concatenate([starts + counts, ends[-1:]]).astype(jnp.int32)
    fill_hi = jnp.concatenate([ends, jnp.full((1,), n_rows, jnp.int32)]).astype(jnp.int32)
    rowmap = _rowmap(dest, fill_lo, fill_hi, n_rows)
    dff = w_down.shape[1]
    ysg = _moe(block_e, first, n_valid, rowmap, h2p, w_gate_up, w_down,
               b_gate_up.reshape(N_EXPERTS, 1, 2 * dff), b_down.reshape(N_EXPERTS, 1, d))
    out = _combine(x2, meta, final_norm_g.reshape(1, d), ysg)
    return out.reshape(b, s, d)
```

```python
import functools
import math

import jax
import jax.numpy as jnp
from jax import lax
from jax.experimental import pallas as pl
from jax.experimental.pallas import tpu as pltpu

F32 = jnp.float32
BF16 = jnp.bfloat16

HEAD_DIM = 64
N_HEADS = 8
N_EXPERTS = 32
TOP_K = 4
LANES = 128
NORM_EPS = 1e-5
SWIGLU_LIMIT = 7.0
SWIGLU_ALPHA = 1.702
LOG2E = math.log2(math.e)
NEG_BIG = -1e30
N_BIAS_PARTS = 3
EXP_UNDERFLOW = 110.0
EXP2_UNDERFLOW = 152.0

TM_PROJ = 512
BQ = 512
BK = 512
SUB = 256
TM_ROUTE = 512
ROWMAP_CHUNK = 4096
TM_COMBINE = 256
BM_MOE = 256
VMEM_LIMIT = 56 * 1024 * 1024


def _cparams(sem):
    return pltpu.CompilerParams(dimension_semantics=sem, vmem_limit_bytes=VMEM_LIMIT)


def _bdot(a, b):
    return jnp.dot(a, b, preferred_element_type=F32)


def _split_bf16(x, parts):
    out = []
    for _ in range(parts):
        piece = x.astype(BF16)
        out.append(piece)
        x = x - piece.astype(F32)
    return out


def _dot3(a_hi, a_lo, b_hi_ref, b_lo_ref):
    return _bdot(a_hi, b_hi_ref[...]) + _bdot(a_lo, b_hi_ref[...]) + _bdot(a_hi, b_lo_ref[...])


def _nt_dot(a, b):
    return lax.dot_general(a, b, (((1,), (1,)), ((), ())), preferred_element_type=F32)


def _inproj_kernel(x_ref, g_ref, w_ref, wf_hi_ref, wf_lo_ref, bf_ref, tri_ref, place_ref,
                   qa_ref, ka_ref, va_ref, qb_ref, kb_ref, vb_ref, carry_ref, *, tiles_per_seq):
    i = pl.program_id(0)

    @pl.when(i % tiles_per_seq == 0)
    def _():
        carry_ref[...] = jnp.zeros_like(carry_ref)

    x = x_ref[...]
    ms = jnp.mean(x * x, axis=-1, keepdims=True)
    h = x * lax.rsqrt(ms + NORM_EPS) * g_ref[...]
    hb, h_lo = _split_bf16(h, 2)

    logit = _dot3(hb, h_lo, wf_hi_ref, wf_lo_ref) + bf_ref[...]
    logf = jnp.minimum(logit, 0.0) - jnp.log(1.0 + jnp.exp(-jnp.abs(logit)))
    c = carry_ref[0:1, :]
    for piece in _split_bf16(logf, 3):
        c = c + _bdot(tri_ref[...], piece)
    tm = c.shape[0]
    carry_ref[...] = jnp.broadcast_to(c[tm - 1:tm, :], carry_ref.shape)
    placed = jnp.zeros((tm, ka_ref.shape[1]), F32)
    for n, part in enumerate(_split_bf16(-LOG2E * c, N_BIAS_PARTS)):
        placed = placed + _bdot(part, place_ref[n])

    lane = lax.broadcasted_iota(jnp.int32, (1, qa_ref.shape[1]), 1) % LANES
    q_extra = jnp.where((lane >= HEAD_DIM) & (lane < HEAD_DIM + N_BIAS_PARTS), 1.0, 0.0)
    v_extra = jnp.where(lane >= HEAD_DIM, 1.0, 0.0)
    extras = (q_extra, placed, v_extra, None, None, None)
    offset = 0
    for ref, extra in zip((qa_ref, ka_ref, va_ref, qb_ref, kb_ref, vb_ref), extras):
        width = ref.shape[1]
        out = _bdot(hb, w_ref[:, offset:offset + width])
        ref[...] = (out if extra is None else out + extra).astype(BF16)
        offset += width


def _pad_heads(w):
    d = w.shape[0]
    w = w.reshape(d, N_HEADS, HEAD_DIM)
    return jnp.concatenate([w, jnp.zeros_like(w)], axis=2).reshape(d, N_HEADS * LANES)


def _inproj(x2d, g, w_all, wf, bf, seq):
    t, d = x2d.shape
    tm = TM_PROJ
    width = N_HEADS * LANES
    tiles_per_seq = seq // tm
    tri = (jnp.arange(tm)[:, None] >= jnp.arange(tm)[None, :]).astype(BF16)
    wf_hi = wf.astype(BF16)
    wf_lo = (wf - wf_hi.astype(F32)).astype(BF16)
    src = jnp.arange(LANES)[:, None]
    dst = jnp.arange(width)[None, :]
    place = jnp.stack([((dst == src * LANES + HEAD_DIM + n) & (src < N_HEADS)).astype(BF16)
                       for n in range(N_BIAS_PARTS)])
    pair_width = N_HEADS * HEAD_DIM
    widths = (width, width, width, pair_width, pair_width, width)
    assert sum(widths) == w_all.shape[1]
    whole = lambda a: pl.BlockSpec(a.shape, lambda i: (0,) * a.ndim)
    return pl.pallas_call(
        functools.partial(_inproj_kernel, tiles_per_seq=tiles_per_seq),
        grid=(t // tm,),
        in_specs=[pl.BlockSpec((tm, d), lambda i: (i, 0)), whole(g), whole(w_all), whole(wf_hi), whole(wf_lo),
                  whole(bf), whole(tri), whole(place)],
        out_specs=[pl.BlockSpec((tm, w), lambda i: (i, 0)) for w in widths],
        out_shape=[jax.ShapeDtypeStruct((t, w), BF16) for w in widths],
        scratch_shapes=[pltpu.VMEM((8, LANES), F32)],
        compiler_params=_cparams(("arbitrary",)),
        name="inproj",
    )(x2d, g, w_all, wf_hi, wf_lo, bf, tri, place)


def _pair_out(o_even, o_odd):
    lane = lax.broadcasted_iota(jnp.int32, o_even.shape, 1)
    return jnp.where(lane < HEAD_DIM, o_even, pltpu.roll(o_odd, HEAD_DIM, axis=1))


def _head_sq_norm(tile):
    tile = tile.astype(F32)
    lane = lax.broadcasted_iota(jnp.int32, tile.shape, 1)
    return jnp.sum(jnp.where(lane < HEAD_DIM, tile * tile, 0.0), axis=1, keepdims=True)


def _fox_kernel(q_ref, k_ref, v_ref, o_ref, kmax_ref, s_ref):
    i = pl.program_id(2)
    bq = q_ref.shape[0]
    n_seq_blocks = k_ref.shape[0] // BK

    @pl.when(i == 0)
    def _():
        def body(n, best):
            start = pl.multiple_of(n * BK, BK)
            return tuple(
                jnp.maximum(best[hh], jnp.max(_head_sq_norm(k_ref[pl.ds(start, BK), hh * LANES:(hh + 1) * LANES]),
                                              axis=0, keepdims=True))
                for hh in range(2))
        best = lax.fori_loop(0, n_seq_blocks, body, (jnp.zeros((1, 1), F32), jnp.zeros((1, 1), F32)))
        for hh in range(2):
            kmax_ref[hh:hh + 1, :] = jnp.broadcast_to(jnp.sqrt(best[hh]), (1, LANES))

    q_heads = (q_ref[:, 0:LANES], q_ref[:, LANES:2 * LANES])
    reach = tuple(jnp.sqrt(_head_sq_norm(q_heads[hh])) * kmax_ref[hh:hh + 1, 0:1] for hh in range(2))
    row = lax.broadcasted_iota(jnp.int32, (bq, BK), 0)
    col = lax.broadcasted_iota(jnp.int32, (bq, BK), 1)
    causal = col <= row
    lane1 = lax.broadcasted_iota(jnp.int32, (1, LANES), 1)
    bias_lanes = (lane1 >= HEAD_DIM) & (lane1 < HEAD_DIM + N_BIAS_PARTS)

    def scores(j, hh):
        start = pl.multiple_of(j * BK, BK)
        return _nt_dot(q_heads[hh], k_ref[pl.ds(start, BK), hh * LANES:(hh + 1) * LANES])

    def look_ahead(j):
        nxt = jnp.maximum(j - 1, 0)
        return [scores(nxt, hh) for hh in range(2)]

    def step(j, carry, s_pair):
        start = pl.multiple_of(j * BK, BK)
        new = []
        slack = None
        for hh in range(2):
            m, acc = carry[hh]
            s = s_pair[hh]
            vs = v_ref[pl.ds(start, BK), hh * LANES:(hh + 1) * LANES]
            m_new = jnp.maximum(m, jnp.max(s, axis=1, keepdims=True))
            p = jnp.exp2(s - m_new)
            acc = jnp.exp2(m - m_new) * acc + _bdot(p.astype(BF16), vs)
            new.append((m_new, acc))
            first = k_ref[pl.ds(start, 1), hh * LANES:(hh + 1) * LANES].astype(F32)
            bias0 = jnp.sum(jnp.where(bias_lanes, first, 0.0), axis=1, keepdims=True)
            bound = jnp.max(reach[hh] + bias0 - m_new)
            slack = bound if slack is None else jnp.maximum(slack, bound)
        return tuple(new), (slack < -EXP2_UNDERFLOW).astype(jnp.int32)

    init = tuple((jnp.full((bq, 1), NEG_BIG, F32), jnp.zeros((bq, LANES), F32)) for _ in range(2))
    ahead = look_ahead(i)
    carry, done = step(i, init, [jnp.where(causal, scores(i, hh), NEG_BIG) for hh in range(2)])
    for hh in range(2):
        s_ref[hh] = ahead[hh]

    def block(state):
        n, _, c = state
        j = i - 1 - n
        s_pair = [s_ref[hh] for hh in range(2)]
        ahead = look_ahead(j)
        c, done = step(j, c, s_pair)
        for hh in range(2):
            s_ref[hh] = ahead[hh]
        return n + 1, done, c

    _, _, carry = lax.while_loop(lambda st: (st[0] < i) & (st[1] == 0), block, (jnp.int32(0), done, carry))
    (_, acc_a), (_, acc_b) = carry
    norm = lambda acc: acc / pltpu.roll(acc, HEAD_DIM, axis=1)
    o_ref[...] = _pair_out(norm(acc_a), norm(acc_b))


def _softplus(z):
    return jnp.maximum(z, 0.0) + jnp.log(1.0 + jnp.exp(-jnp.abs(z)))


def _sb_kernel(q_ref, k_ref, v_ref, tri_ref, o_ref):
    i = pl.program_id(2)
    bq = q_ref.shape[0]
    n_sub = bq // SUB
    q2 = q_ref[...]
    lane_q = lax.broadcasted_iota(jnp.int32, q2.shape, 1)
    q_heads = (jnp.where(lane_q < HEAD_DIM, q2, jnp.zeros_like(q2)),
               jnp.where(lane_q >= HEAD_DIM, q2, jnp.zeros_like(q2)))
    row = lax.broadcasted_iota(jnp.int32, (SUB, SUB), 0)
    col = lax.broadcasted_iota(jnp.int32, (SUB, SUB), 1)
    strict = col < row
    tri = tri_ref[...]

    def sub_step(start, carry, first_row):
        r0 = 0 if first_row is None else first_row

        def mask_top(x):
            top = jnp.where(strict, x[:SUB], 0.0)
            return top if x.shape[0] == SUB else jnp.concatenate([top, x[SUB:]], axis=0)

        def add_rows(full, part):
            return full + part if r0 == 0 else jnp.concatenate([full[:r0], full[r0:] + part], axis=0)

        ks = k_ref[pl.ds(start, SUB), :]
        new = []
        for hh in range(2):
            later, acc = carry[hh]
            vs = v_ref[pl.ds(start, SUB), hh * LANES:(hh + 1) * LANES]
            z = _nt_dot(q_heads[hh][r0:], ks)
            sp = _softplus(z)
            if first_row is not None:
                sp = mask_top(sp)
            hi = sp.astype(BF16)
            lo = (sp - hi.astype(F32)).astype(BF16)
            g = _bdot(hi, tri) + _bdot(lo, tri)
            a = jnp.exp(z - g - later[r0:])
            if first_row is not None:
                a = mask_top(a)
            new.append((add_rows(later, g[:, 0:1]), add_rows(acc, _bdot(a.astype(BF16), vs))))
        return tuple(new)

    carry = tuple((jnp.zeros((bq, 1), F32), jnp.zeros((bq, LANES), F32)) for _ in range(2))
    for u in reversed(range(n_sub)):
        carry = sub_step(pl.multiple_of(i * bq + u * SUB, SUB), carry, u * SUB)

    def decayed(c):
        return (jnp.min(jnp.minimum(c[0][0], c[1][0])) >= EXP_UNDERFLOW).astype(jnp.int32)

    def back(state):
        n, _, c = state
        c = sub_step(pl.multiple_of(i * bq - (n + 1) * SUB, SUB), c, None)
        return n + 1, decayed(c), c

    _, _, carry = lax.while_loop(lambda st: (st[0] < i * n_sub) & (st[1] == 0), back,
                                 (jnp.int32(0), decayed(carry), carry))
    o_ref[...] = _pair_out(carry[0][1], carry[1][1])


def _attention_call(body, name, q, k, v, extra, seq, q_lanes, k_lanes, scratch=()):
    t = q.shape[0]
    nb, nq, npair = t // seq, seq // BQ, N_HEADS // 2
    q_spec = lambda w: pl.BlockSpec((BQ, w), lambda b, p, i: (b * nq + i, p))
    kv_spec = lambda w: pl.BlockSpec((seq, w), lambda b, p, i: (b, p))
    return pl.pallas_call(
        body,
        grid=(nb, npair, nq),
        in_specs=[q_spec(q_lanes), kv_spec(k_lanes), kv_spec(2 * LANES)]
        + [pl.BlockSpec(a.shape, lambda b, p, i: (0,) * a.ndim) for a in extra],
        out_specs=q_spec(LANES),
        out_shape=jax.ShapeDtypeStruct((t, npair * LANES), F32),
        scratch_shapes=list(scratch),
        compiler_params=_cparams(("arbitrary", "arbitrary", "arbitrary")),
        name=name,
    )(q, k, v, *extra)


def _rms(y, g):
    return y * lax.rsqrt(jnp.mean(y * y, axis=-1, keepdims=True) + NORM_EPS) * g


def _as_u32(words):
    return words if words.dtype == jnp.uint32 else pltpu.bitcast(words, jnp.uint32)


def _pack_bf16_pairs(x):
    n = x.shape[1] // 2
    return _as_u32(pltpu.pack_elementwise([x[:, :n], x[:, n:]], packed_dtype=BF16))


def _unpack_bf16_pairs(p):
    halves = [pltpu.unpack_elementwise(p, index=n, packed_dtype=BF16, unpacked_dtype=F32) for n in range(2)]
    return jnp.concatenate(halves, axis=1)


def _outproj_kernel(x_ref, ya_ref, yb_ref, ga_ref, gb_ref, wo_ref, gf_ref, wr_hi_ref, wr_lo_ref, br_ref,
                    ltri_ref, x2_ref, h2p_ref, meta_ref, cnt_ref, carry_ref):
    i = pl.program_id(0)

    @pl.when(i == 0)
    def _():
        carry_ref[...] = jnp.zeros_like(carry_ref)

    ya = _rms(ya_ref[...], ga_ref[...]).astype(BF16)
    yb = _rms(yb_ref[...], gb_ref[...]).astype(BF16)
    wa = ya_ref.shape[1]
    x2 = x_ref[...] + _bdot(ya, wo_ref[0:wa, :]) + _bdot(yb, wo_ref[wa:, :])
    x2_ref[...] = x2
    h2 = _rms(x2, gf_ref[...])
    h2p_ref[...] = _pack_bf16_pairs(h2)

    h2_hi, h2_lo = _split_bf16(h2, 2)
    logits = _dot3(h2_hi, h2_lo, wr_hi_ref, wr_lo_ref) + br_ref[...]
    tm = logits.shape[0]
    lane = lax.broadcasted_iota(jnp.int32, (tm, LANES), 1)
    lane_f = lane.astype(F32)
    work = logits
    vals, sels, idxs = [], [], []
    for _ in range(TOP_K):
        mx = jnp.max(work, axis=1, keepdims=True)
        idx = jnp.min(jnp.where(work == mx, lane_f, float(LANES)), axis=1, keepdims=True)
        sel = lane_f == idx
        vals.append(mx)
        idxs.append(idx)
        sels.append(sel)
        work = jnp.where(sel, -jnp.inf, work)
    exps = [jnp.exp(v - vals[0]) for v in vals]
    denom = exps[0] + exps[1] + exps[2] + exps[3]
    gates = [e / denom for e in exps]

    onehot = jnp.zeros((tm, LANES), F32)
    for sel in sels:
        onehot = onehot + jnp.where(sel, 1.0, 0.0)
    before = _bdot(ltri_ref[...], onehot.astype(BF16)) + carry_ref[0:1, :]
    meta = jnp.zeros((tm, LANES), F32)
    for k in range(TOP_K):
        rank = jnp.sum(jnp.where(sels[k], before, 0.0), axis=1, keepdims=True)
        meta = jnp.where(lane == k, idxs[k], meta)
        meta = jnp.where(lane == TOP_K + k, rank, meta)
        meta = jnp.where(lane == 2 * TOP_K + k, gates[k], meta)
    meta_ref[...] = meta
    total = carry_ref[0:1, :] + jnp.sum(onehot, axis=0, keepdims=True)
    carry_ref[...] = jnp.broadcast_to(total, carry_ref.shape)
    cnt_ref[...] = jnp.broadcast_to(total, cnt_ref.shape)


def _outproj(x2d, ya, yb, ga, gb, wo, gf, wr, br):
    t, d = x2d.shape
    tm = TM_ROUTE
    wa = ya.shape[1]
    ltri = (jnp.arange(tm)[:, None] > jnp.arange(tm)[None, :]).astype(BF16)
    wr_hi = wr.astype(BF16)
    wr_lo = (wr - wr_hi.astype(F32)).astype(BF16)
    tile = lambda w: pl.BlockSpec((tm, w), lambda i: (i, 0))
    whole = lambda a: pl.BlockSpec(a.shape, lambda i: (0, 0))
    return pl.pallas_call(
        _outproj_kernel,
        grid=(t // tm,),
        in_specs=[tile(d), tile(wa), tile(wa), whole(ga), whole(gb), whole(wo), whole(gf),
                  whole(wr_hi), whole(wr_lo), whole(br), whole(ltri)],
        out_specs=[tile(d), tile(d // 2), tile(LANES), pl.BlockSpec((8, LANES), lambda i: (0, 0))],
        out_shape=[jax.ShapeDtypeStruct((t, d), F32), jax.ShapeDtypeStruct((t, d // 2), jnp.uint32),
                   jax.ShapeDtypeStruct((t, LANES), F32), jax.ShapeDtypeStruct((8, LANES), F32)],
        scratch_shapes=[pltpu.VMEM((8, LANES), F32)],
        compiler_params=_cparams(("arbitrary",)),
        name="outproj",
    )(x2d, ya, yb, ga, gb, wo, gf, wr_hi, wr_lo, br, ltri)


def _rowmap_kernel(fill_lo_ref, fill_hi_ref, dest_ref, map_ref, *, n_slots, bm):
    n = dest_ref.shape[0]
    step = pl.program_id(0)

    @pl.when(step == 0)
    def _():
        def fill_range(e, _):
            def fill(j, _):
                map_ref[j] = n_slots + (j & (2 * bm - 1))
                return 0
            return lax.fori_loop(fill_lo_ref[e], fill_hi_ref[e], fill, 0)
        lax.fori_loop(0, fill_lo_ref.shape[0], fill_range, 0)

    def body(j, _):
        map_ref[dest_ref[j]] = step * n + j
        return 0

    lax.fori_loop(0, n, body, 0, unroll=8)


def _rowmap(dest_flat, fill_lo, fill_hi, n_rows):
    n = dest_flat.shape[0]
    chunk = ROWMAP_CHUNK
    assert BM_MOE & (BM_MOE - 1) == 0
    grid_spec = pltpu.PrefetchScalarGridSpec(
        num_scalar_prefetch=2,
        grid=(n // chunk,),
        in_specs=[pl.BlockSpec((chunk,), lambda i, *_: (i,), memory_space=pltpu.SMEM)],
        out_specs=pl.BlockSpec((n_rows,), lambda i, *_: (0,), memory_space=pltpu.SMEM),
    )
    return pl.pallas_call(
        functools.partial(_rowmap_kernel, n_slots=n, bm=BM_MOE),
        grid_spec=grid_spec,
        out_shape=jax.ShapeDtypeStruct((n_rows,), jnp.int32),
        compiler_params=_cparams(("arbitrary",)),
        name="rowmap",
    )(fill_lo, fill_hi, dest_flat)


def _moe_kernel(be_ref, first_ref, nv_ref, wslot_ref, nexte_ref, s0_ref, sprev_ref, snext_ref, h_ref, wgu_ref, wd_ref,
                bgu_ref, bd_ref, ysg_ref, wgu_bf_ref, wd2_ref, wgu_f32_ref, wd_f32_ref, x0_ref, x1_ref,
                y0_ref, y1_ref, gsem, ssem, wsem):
    i = pl.program_id(0)
    nv = nv_ref[0]
    bm = x0_ref.shape[0]
    n_tokens = h_ref.shape[0]
    xbufs, ybufs = (x0_ref, x1_ref), (y0_ref, y1_ref)

    def gather_row(slots_ref, r, buf):
        token = slots_ref[r] & (n_tokens - 1)
        return pltpu.make_async_copy(h_ref.at[pl.ds(token, 1), :], xbufs[buf].at[pl.ds(r, 1), :], gsem.at[buf])

    def scatter_row(slots_ref, r, buf):
        return pltpu.make_async_copy(ybufs[buf].at[pl.ds(r, 1), :], ysg_ref.at[pl.ds(slots_ref[r], 1), :],
                                     ssem.at[buf])

    def spare_row(r, buf):
        return pltpu.make_async_copy(ybufs[buf].at[pl.ds(r, 1), :],
                                     ysg_ref.at[pl.ds(ysg_ref.shape[0] - bm + r, 1), :], ssem.at[buf])

    @pl.when(i == 0)
    def _():
        y0_ref[...] = jnp.zeros_like(y0_ref)
        y1_ref[...] = jnp.zeros_like(y1_ref)
        for r in range(bm):
            spare_row(r, 0).start()
        for r in range(bm):
            gather_row(s0_ref, r, 0).start()

    def weight_copies(expert, slot):
        return (pltpu.make_async_copy(wgu_ref.at[expert], wgu_f32_ref.at[slot], wsem.at[slot]),
                pltpu.make_async_copy(wd_ref.at[expert], wd_f32_ref.at[slot], wsem.at[slot]))

    @pl.when(i == 0)
    def _():
        for copy in weight_copies(be_ref[0], 0):
            copy.start(priority=1)

    @pl.when(first_ref[i] == 1)
    def _():
        slot = wslot_ref[i]
        for copy in weight_copies(be_ref[i], slot):
            copy.wait()
        wgu_bf_ref[...] = wgu_f32_ref[slot].astype(BF16)
        wd = wd_f32_ref[slot]
        wd2_ref[...] = _as_u32(pltpu.pack_elementwise([wd, wd], packed_dtype=BF16))

        @pl.when(nexte_ref[i] >= 0)
        def _():
            for copy in weight_copies(nexte_ref[i], 1 - slot):
                copy.start(priority=1)

    def compute(cur):
        nxt = 1 - cur
        for r in range(bm):
            gather_row(snext_ref, r, cur).wait()
        xb = _unpack_bf16_pairs(xbufs[cur][...]).astype(BF16)
        for r in range(bm):
            gather_row(snext_ref, r, nxt).start()
        for r in range(bm):
            scatter_row(sprev_ref, r, nxt).start(priority=1)
        gu = _bdot(xb, wgu_bf_ref[...]) + bgu_ref[0]
        even = lax.broadcasted_iota(jnp.int32, (gu.shape[0], LANES), 1) % 2 == 0
        acts = []
        for c in range(gu.shape[1] // LANES):
            blk = gu[:, c * LANES:(c + 1) * LANES]
            gate = jnp.minimum(blk, SWIGLU_LIMIT)
            glu = gate * (1.0 / (1.0 + jnp.exp(-SWIGLU_ALPHA * gate)))
            up1 = jnp.clip(blk, -SWIGLU_LIMIT, SWIGLU_LIMIT) + 1.0
            acts.append(jnp.where(even, glu * pltpu.roll(up1, LANES - 1, axis=1), 0.0).astype(BF16))
        act = jnp.concatenate(acts, axis=1)
        wd2 = pltpu.bitcast(wd2_ref[...], BF16)
        y = _pack_bf16_pairs(_bdot(act, wd2) + bd_ref[0])
        for r in range(bm):
            scatter_row(sprev_ref, r, cur).wait()
        ybufs[cur][...] = y

    def drain(cur):
        nxt = 1 - cur
        for r in range(bm):
            scatter_row(sprev_ref, r, cur).wait()
        for r in range(bm):
            scatter_row(sprev_ref, r, nxt).start()
        for r in range(bm):
            scatter_row(sprev_ref, r, nxt).wait()
        for r in range(bm):
            gather_row(snext_ref, r, cur).wait()

    for cur in range(2):
        parity = (i % 2) == cur
        pl.when(parity & (i < nv))(functools.partial(compute, cur))
        pl.when(parity & (i == nv))(functools.partial(drain, cur))


def _moe(block_e, first, n_valid, rowmap, h2p, wgu, wd, bgu, bd):
    t, half = h2p.shape
    assert t & (t - 1) == 0, "token count must be a power of two (slot -> token uses a bit mask)"
    bm = BM_MOE
    n_blocks = rowmap.shape[0] // bm - 1
    dff, d = wd.shape[1:]
    n_steps = block_e.shape[0]
    wslot = ((jnp.cumsum(first) - 1) % 2).astype(jnp.int32)
    step = jnp.arange(n_steps, dtype=jnp.int32)
    later_first = jnp.where((first[None, :] == 1) & (step[None, :] > step[:, None]), step[None, :], n_steps)
    next_first = jnp.min(later_first, axis=1)
    next_e = jnp.where(next_first < n_steps, block_e[jnp.minimum(next_first, n_steps - 1)], -1).astype(jnp.int32)

    b_spec = lambda a: pl.BlockSpec((1,) + a.shape[1:], lambda i, be, *_: (be[i], 0, 0))
    slots = lambda index: pl.BlockSpec((bm,), index, memory_space=pltpu.SMEM)
    tile = pltpu.VMEM((bm, half), jnp.uint32)
    grid_spec = pltpu.PrefetchScalarGridSpec(
        num_scalar_prefetch=5,
        grid=(n_blocks + 1,),
        in_specs=[slots(lambda i, *_: (1,)),
                  slots(lambda i, *_: (i,)),
                  slots(lambda i, *_: (jnp.minimum(i + 2, n_blocks),)),
                  pl.BlockSpec(memory_space=pl.ANY), pl.BlockSpec(memory_space=pl.ANY),
                  pl.BlockSpec(memory_space=pl.ANY), b_spec(bgu), b_spec(bd)],
        out_specs=pl.BlockSpec(memory_space=pl.ANY),
        scratch_shapes=[pltpu.VMEM((d, 2 * dff), BF16), pltpu.VMEM((dff, d), jnp.uint32),
                        pltpu.VMEM((2, d, 2 * dff), F32), pltpu.VMEM((2, dff, d), F32),
                        tile, tile, tile, tile,
                        pltpu.SemaphoreType.DMA((2,)), pltpu.SemaphoreType.DMA((2,)),
                        pltpu.SemaphoreType.DMA((2,))],
    )
    return pl.pallas_call(
        _moe_kernel,
        grid_spec=grid_spec,
        out_shape=jax.ShapeDtypeStruct((TOP_K * t + 2 * bm, half), jnp.uint32),
        compiler_params=_cparams(("arbitrary",)),
        name="moe",
    )(block_e, first, n_valid, wslot, next_e, rowmap, rowmap, rowmap, h2p, wgu, wd, bgu, bd)


def _combine_kernel(x2_ref, meta_ref, g_ref, *rest):
    ys_refs, o_ref = rest[:TOP_K], rest[TOP_K]
    meta = meta_ref[...]
    y = x2_ref[...]
    for k in range(TOP_K):
        y = y + meta[:, 2 * TOP_K + k:2 * TOP_K + k + 1] * _unpack_bf16_pairs(ys_refs[k][...])
    o_ref[...] = _rms(y, g_ref[...])


def _combine(x2, meta, g, ysg):
    t, d = x2.shape
    tm = TM_COMBINE
    tiles = t // tm
    slot_spec = lambda k: pl.BlockSpec((tm, ysg.shape[1]), lambda i: (k * tiles + i, 0))
    return pl.pallas_call(
        _combine_kernel,
        grid=(tiles,),
        in_specs=[pl.BlockSpec((tm, d), lambda i: (i, 0)),
                  pl.BlockSpec((tm, LANES), lambda i: (i, 0)),
                  pl.BlockSpec((1, d), lambda i: (0, 0))] + [slot_spec(k) for k in range(TOP_K)],
        out_specs=pl.BlockSpec((tm, d), lambda i: (i, 0)),
        out_shape=jax.ShapeDtypeStruct((t, d), F32),
        compiler_params=_cparams(("arbitrary",)),
        name="combine",
    )(x2, meta, g, *([ysg] * TOP_K))


def kernel(x, attn_norm_g, w_in, b_forget, fox_out_g, sb_out_g, w_out, ffn_norm_g, w_router, b_router,
           w_gate_up, b_gate_up, w_down, b_down, final_norm_g):
    b, s, d = x.shape
    t = b * s
    fw = N_HEADS * HEAD_DIM
    x2d = x.reshape(t, d)

    o = 0
    parts = []
    for width in (fw, fw, fw, N_HEADS, fw, fw, fw):
        parts.append(w_in[:, o:o + width])
        o += width
    wqa, wka, wva, wf, wqb, wkb, wvb = parts
    scale = 1.0 / math.sqrt(HEAD_DIM)
    w_all = jnp.concatenate(
        [_pad_heads(wqa * (scale * LOG2E)), _pad_heads(wka), _pad_heads(wva),
         wqb * scale, wkb, _pad_heads(wvb)], axis=1).astype(BF16)
    wf_pad = jnp.zeros((d, LANES), F32).at[:, :N_HEADS].set(wf)
    bf_pad = jnp.zeros((1, LANES), F32).at[0, :N_HEADS].set(b_forget)

    qa, ka, va, qb, kb, vb = _inproj(x2d, attn_norm_g.reshape(1, d), w_all, wf_pad, bf_pad, s)
    ya = _attention_call(_fox_kernel, "fox", qa, ka, va, (), s, 2 * LANES, 2 * LANES,
                         scratch=(pltpu.VMEM((8, LANES), F32), pltpu.VMEM((2, BQ, BK), F32)))
    tri = (jnp.arange(SUB)[:, None] >= jnp.arange(SUB)[None, :]).astype(BF16)
    yb = _attention_call(_sb_kernel, "sb", qb, kb, vb, (tri,), s, LANES, LANES)

    wr = jnp.zeros((d, LANES), F32).at[:, :N_EXPERTS].set(w_router)
    br = jnp.full((1, LANES), NEG_BIG, F32).at[0, :N_EXPERTS].set(b_router)
    x2, h2p, meta, cnt = _outproj(x2d, ya, yb, fox_out_g.reshape(1, fw), sb_out_g.reshape(1, fw),
                                 w_out.astype(BF16), ffn_norm_g.reshape(1, d), wr, br)

    bm = BM_MOE
    n_blocks = (t * TOP_K + N_EXPERTS * (bm - 1)) // bm
    counts = cnt[0, :N_EXPERTS].astype(jnp.int32)
    padded = (counts + bm - 1) // bm * bm
    ends = jnp.cumsum(padded)
    starts = ends - padded
    top_idx = meta[:, 0:TOP_K].astype(jnp.int32)
    rank = meta[:, TOP_K:2 * TOP_K].astype(jnp.int32)
    dest = (starts[top_idx] + rank).T.reshape(t * TOP_K)
    block_start = jnp.arange(n_blocks + 1, dtype=jnp.int32) * bm
    block_e = jnp.minimum(jnp.sum(block_start[:, None] >= ends[None, :], axis=1), N_EXPERTS - 1).astype(jnp.int32)
    first = jnp.concatenate([jnp.ones((1,), jnp.int32), (block_e[1:] != block_e[:-1]).astype(jnp.int32)])
    n_valid = (ends[-1:] // bm).astype(jnp.int32)

    n_rows = (n_blocks + 1) * bm
    lead = jnp.zeros((1,), jnp.int32)
    fill_lo = jnp.concatenate([lead, bm + starts + counts, bm + ends[-1:]]).astype(jnp.int32)
    fill_hi = jnp.concatenate([lead + bm, bm + ends, jnp.full((1,), n_rows, jnp.int32)]).astype(jnp.int32)
    rowmap = _rowmap(dest + bm, fill_lo, fill_hi, n_rows)
    dff = w_down.shape[1]
    ysg = _moe(block_e, first, n_valid, rowmap, h2p, w_gate_up, w_down,
               b_gate_up.reshape(N_EXPERTS, 1, 2 * dff), b_down.reshape(N_EXPERTS, 1, d))
    out = _combine(x2, meta, final_norm_g.reshape(1, d), ysg)
    return out.reshape(b, s, d)
```

```python
import functools
import math

import jax
import jax.numpy as jnp
from jax import lax
from jax.experimental import pallas as pl
from jax.experimental.pallas import tpu as pltpu

F32 = jnp.float32
BF16 = jnp.bfloat16

HEAD_DIM = 64
N_HEADS = 8
N_EXPERTS = 32
TOP_K = 4
LANES = 128
NORM_EPS = 1e-5
SWIGLU_LIMIT = 7.0
SWIGLU_ALPHA = 1.702
LOG2E = math.log2(math.e)
NEG_BIG = -1e30
N_BIAS_PARTS = 3
EXP_UNDERFLOW = 110.0
EXP2_UNDERFLOW = 152.0

TM_PROJ = 512
BQ = 512
BK = 512
SUB = 256
SB_PAIRS = 2
TM_ROUTE = 512
ROWMAP_CHUNK = 4096
TM_COMBINE = 256
BM_MOE = 256
VMEM_LIMIT = 56 * 1024 * 1024


def _cparams(sem):
    return pltpu.CompilerParams(dimension_semantics=sem, vmem_limit_bytes=VMEM_LIMIT)


def _bdot(a, b):
    return jnp.dot(a, b, preferred_element_type=F32)


def _split_bf16(x, parts):
    out = []
    for _ in range(parts):
        piece = x.astype(BF16)
        out.append(piece)
        x = x - piece.astype(F32)
    return out


def _dot3(a_hi, a_lo, b_hi_ref, b_lo_ref):
    return _bdot(a_hi, b_hi_ref[...]) + _bdot(a_lo, b_hi_ref[...]) + _bdot(a_hi, b_lo_ref[...])


def _nt_dot(a, b):
    return lax.dot_general(a, b, (((1,), (1,)), ((), ())), preferred_element_type=F32)


def _inproj_kernel(x_ref, g_ref, w_ref, wf_hi_ref, wf_lo_ref, bf_ref, tri_ref, place_ref,
                   qa_ref, ka_ref, va_ref, qb_ref, kb_ref, vb_ref, carry_ref, *, tiles_per_seq):
    i = pl.program_id(0)

    @pl.when(i % tiles_per_seq == 0)
    def _():
        carry_ref[...] = jnp.zeros_like(carry_ref)

    x = x_ref[...]
    ms = jnp.mean(x * x, axis=-1, keepdims=True)
    h = x * lax.rsqrt(ms + NORM_EPS) * g_ref[...]
    hb, h_lo = _split_bf16(h, 2)

    logit = _dot3(hb, h_lo, wf_hi_ref, wf_lo_ref) + bf_ref[...]
    logf = jnp.minimum(logit, 0.0) - jnp.log(1.0 + jnp.exp(-jnp.abs(logit)))
    c = carry_ref[0:1, :]
    for piece in _split_bf16(logf, 3):
        c = c + _bdot(tri_ref[...], piece)
    tm = c.shape[0]
    carry_ref[...] = jnp.broadcast_to(c[tm - 1:tm, :], carry_ref.shape)
    placed = jnp.zeros((tm, ka_ref.shape[1]), F32)
    for n, part in enumerate(_split_bf16(-LOG2E * c, N_BIAS_PARTS)):
        placed = placed + _bdot(part, place_ref[n])

    lane = lax.broadcasted_iota(jnp.int32, (1, qa_ref.shape[1]), 1) % LANES
    q_extra = jnp.where((lane >= HEAD_DIM) & (lane < HEAD_DIM + N_BIAS_PARTS), 1.0, 0.0)
    v_extra = jnp.where(lane >= HEAD_DIM, 1.0, 0.0)
    extras = (q_extra, placed, v_extra, None, None, None)
    offset = 0
    for ref, extra in zip((qa_ref, ka_ref, va_ref, qb_ref, kb_ref, vb_ref), extras):
        width = ref.shape[1]
        out = _bdot(hb, w_ref[:, offset:offset + width])
        ref[...] = (out if extra is None else out + extra).astype(BF16)
        offset += width


def _pad_heads(w):
    d = w.shape[0]
    w = w.reshape(d, N_HEADS, HEAD_DIM)
    return jnp.concatenate([w, jnp.zeros_like(w)], axis=2).reshape(d, N_HEADS * LANES)


def _inproj(x2d, g, w_all, wf, bf, seq):
    t, d = x2d.shape
    tm = TM_PROJ
    width = N_HEADS * LANES
    tiles_per_seq = seq // tm
    tri = (jnp.arange(tm)[:, None] >= jnp.arange(tm)[None, :]).astype(BF16)
    wf_hi = wf.astype(BF16)
    wf_lo = (wf - wf_hi.astype(F32)).astype(BF16)
    src = jnp.arange(LANES)[:, None]
    dst = jnp.arange(width)[None, :]
    place = jnp.stack([((dst == src * LANES + HEAD_DIM + n) & (src < N_HEADS)).astype(BF16)
                       for n in range(N_BIAS_PARTS)])
    pair_width = N_HEADS * HEAD_DIM
    widths = (width, width, width, pair_width, pair_width, width)
    assert sum(widths) == w_all.shape[1]
    whole = lambda a: pl.BlockSpec(a.shape, lambda i: (0,) * a.ndim)
    return pl.pallas_call(
        functools.partial(_inproj_kernel, tiles_per_seq=tiles_per_seq),
        grid=(t // tm,),
        in_specs=[pl.BlockSpec((tm, d), lambda i: (i, 0)), whole(g), whole(w_all), whole(wf_hi), whole(wf_lo),
                  whole(bf), whole(tri), whole(place)],
        out_specs=[pl.BlockSpec((tm, w), lambda i: (i, 0)) for w in widths],
        out_shape=[jax.ShapeDtypeStruct((t, w), BF16) for w in widths],
        scratch_shapes=[pltpu.VMEM((8, LANES), F32)],
        compiler_params=_cparams(("arbitrary",)),
        name="inproj",
    )(x2d, g, w_all, wf_hi, wf_lo, bf, tri, place)


def _pair_out(o_even, o_odd):
    lane = lax.broadcasted_iota(jnp.int32, o_even.shape, 1)
    return jnp.where(lane < HEAD_DIM, o_even, pltpu.roll(o_odd, HEAD_DIM, axis=1))


def _head_sq_norm(tile):
    tile = tile.astype(F32)
    lane = lax.broadcasted_iota(jnp.int32, tile.shape, 1)
    return jnp.sum(jnp.where(lane < HEAD_DIM, tile * tile, 0.0), axis=1, keepdims=True)


def _fox_kernel(q_ref, k_ref, v_ref, o_ref, kmax_ref, s_ref):
    i = pl.program_id(2)
    bq = q_ref.shape[0]
    n_seq_blocks = k_ref.shape[0] // BK

    @pl.when(i == 0)
    def _():
        def body(n, best):
            start = pl.multiple_of(n * BK, BK)
            return tuple(
                jnp.maximum(best[hh], jnp.max(_head_sq_norm(k_ref[pl.ds(start, BK), hh * LANES:(hh + 1) * LANES]),
                                              axis=0, keepdims=True))
                for hh in range(2))
        best = lax.fori_loop(0, n_seq_blocks, body, (jnp.zeros((1, 1), F32), jnp.zeros((1, 1), F32)))
        for hh in range(2):
            kmax_ref[hh:hh + 1, :] = jnp.broadcast_to(jnp.sqrt(best[hh]), (1, LANES))

    q_heads = (q_ref[:, 0:LANES], q_ref[:, LANES:2 * LANES])
    reach = tuple(jnp.sqrt(_head_sq_norm(q_heads[hh])) * kmax_ref[hh:hh + 1, 0:1] for hh in range(2))
    row = lax.broadcasted_iota(jnp.int32, (bq, BK), 0)
    col = lax.broadcasted_iota(jnp.int32, (bq, BK), 1)
    causal = col <= row
    lane1 = lax.broadcasted_iota(jnp.int32, (1, LANES), 1)
    bias_lanes = (lane1 >= HEAD_DIM) & (lane1 < HEAD_DIM + N_BIAS_PARTS)

    def scores(j, hh):
        start = pl.multiple_of(j * BK, BK)
        return _nt_dot(q_heads[hh], k_ref[pl.ds(start, BK), hh * LANES:(hh + 1) * LANES])

    def look_ahead(j):
        nxt = jnp.maximum(j - 1, 0)
        return [scores(nxt, hh) for hh in range(2)]

    def step(j, carry, s_pair):
        start = pl.multiple_of(j * BK, BK)
        new = []
        slack = None
        for hh in range(2):
            m, acc = carry[hh]
            s = s_pair[hh]
            vs = v_ref[pl.ds(start, BK), hh * LANES:(hh + 1) * LANES]
            m_new = jnp.maximum(m, jnp.max(s, axis=1, keepdims=True))
            p = jnp.exp2(s - m_new)
            acc = jnp.exp2(m - m_new) * acc + _bdot(p.astype(BF16), vs)
            new.append((m_new, acc))
            first = k_ref[pl.ds(start, 1), hh * LANES:(hh + 1) * LANES].astype(F32)
            bias0 = jnp.sum(jnp.where(bias_lanes, first, 0.0), axis=1, keepdims=True)
            bound = jnp.max(reach[hh] + bias0 - m_new)
            slack = bound if slack is None else jnp.maximum(slack, bound)
        return tuple(new), (slack < -EXP2_UNDERFLOW).astype(jnp.int32)

    init = tuple((jnp.full((bq, 1), NEG_BIG, F32), jnp.zeros((bq, LANES), F32)) for _ in range(2))
    ahead = look_ahead(i)
    carry, done = step(i, init, [jnp.where(causal, scores(i, hh), NEG_BIG) for hh in range(2)])
    for hh in range(2):
        s_ref[hh] = ahead[hh]

    def block(state):
        n, _, c = state
        j = i - 1 - n
        s_pair = [s_ref[hh] for hh in range(2)]
        ahead = look_ahead(j)
        c, done = step(j, c, s_pair)
        for hh in range(2):
            s_ref[hh] = ahead[hh]
        return n + 1, done, c

    _, _, carry = lax.while_loop(lambda st: (st[0] < i) & (st[1] == 0), block, (jnp.int32(0), done, carry))
    (_, acc_a), (_, acc_b) = carry
    norm = lambda acc: acc / pltpu.roll(acc, HEAD_DIM, axis=1)
    o_ref[...] = _pair_out(norm(acc_a), norm(acc_b))


def _softplus(z):
    return jnp.maximum(z, 0.0) + jnp.log(1.0 + jnp.exp(-jnp.abs(z)))


def _sb_kernel(q_ref, k_ref, v_ref, tri_ref, o_ref):
    i = pl.program_id(2)
    bq = q_ref.shape[0]
    n_sub = bq // SUB
    n_heads = v_ref.shape[1] // LANES
    lane_q = lax.broadcasted_iota(jnp.int32, (bq, LANES), 1)
    q_heads = []
    for pp in range(n_heads // 2):
        q2 = q_ref[:, pp * LANES:(pp + 1) * LANES]
        q_heads += [jnp.where(lane_q < HEAD_DIM, q2, jnp.zeros_like(q2)),
                    jnp.where(lane_q >= HEAD_DIM, q2, jnp.zeros_like(q2))]
    row = lax.broadcasted_iota(jnp.int32, (SUB, SUB), 0)
    col = lax.broadcasted_iota(jnp.int32, (SUB, SUB), 1)
    strict = col < row
    tri = tri_ref[...]

    def sub_step(start, carry, first_row):
        r0 = 0 if first_row is None else first_row

        def mask_top(x):
            top = jnp.where(strict, x[:SUB], 0.0)
            return top if x.shape[0] == SUB else jnp.concatenate([top, x[SUB:]], axis=0)

        def add_rows(full, part):
            return full + part if r0 == 0 else jnp.concatenate([full[:r0], full[r0:] + part], axis=0)

        new = []
        for hh in range(n_heads):
            later, acc = carry[hh]
            ks = k_ref[pl.ds(start, SUB), (hh // 2) * LANES:(hh // 2 + 1) * LANES]
            vs = v_ref[pl.ds(start, SUB), hh * LANES:(hh + 1) * LANES]
            z = _nt_dot(q_heads[hh][r0:], ks)
            sp = _softplus(z)
            if first_row is not None:
                sp = mask_top(sp)
            hi = sp.astype(BF16)
            lo = (sp - hi.astype(F32)).astype(BF16)
            g = _bdot(hi, tri) + _bdot(lo, tri)
            a = jnp.exp(z - g - later[r0:])
            if first_row is not None:
                a = mask_top(a)
            new.append((add_rows(later, g[:, 0:1]), add_rows(acc, _bdot(a.astype(BF16), vs))))
        return tuple(new)

    carry = tuple((jnp.zeros((bq, 1), F32), jnp.zeros((bq, LANES), F32)) for _ in range(n_heads))
    for u in reversed(range(n_sub)):
        carry = sub_step(pl.multiple_of(i * bq + u * SUB, SUB), carry, u * SUB)

    def decayed(c):
        least = functools.reduce(jnp.minimum, [head[0] for head in c])
        return (jnp.min(least) >= EXP_UNDERFLOW).astype(jnp.int32)

    def back(state):
        n, _, c = state
        c = sub_step(pl.multiple_of(i * bq - (n + 1) * SUB, SUB), c, None)
        return n + 1, decayed(c), c

    _, _, carry = lax.while_loop(lambda st: (st[0] < i * n_sub) & (st[1] == 0), back,
                                 (jnp.int32(0), decayed(carry), carry))
    for pp in range(n_heads // 2):
        o_ref[:, pp * LANES:(pp + 1) * LANES] = _pair_out(carry[2 * pp][1], carry[2 * pp + 1][1])


def _attention_call(body, name, q, k, v, extra, seq, q_lanes, k_lanes, scratch=(), pairs_per_step=1):
    t = q.shape[0]
    g = pairs_per_step
    nb, nq, npair = t // seq, seq // BQ, N_HEADS // 2
    q_spec = lambda w: pl.BlockSpec((BQ, w * g), lambda b, p, i: (b * nq + i, p))
    kv_spec = lambda w: pl.BlockSpec((seq, w * g), lambda b, p, i: (b, p))
    return pl.pallas_call(
        body,
        grid=(nb, npair // g, nq),
        in_specs=[q_spec(q_lanes), kv_spec(k_lanes), kv_spec(2 * LANES)]
        + [pl.BlockSpec(a.shape, lambda b, p, i: (0,) * a.ndim) for a in extra],
        out_specs=q_spec(LANES),
        out_shape=jax.ShapeDtypeStruct((t, npair * LANES), F32),
        scratch_shapes=list(scratch),
        compiler_params=_cparams(("arbitrary", "arbitrary", "arbitrary")),
        name=name,
    )(q, k, v, *extra)


def _rms(y, g):
    return y * lax.rsqrt(jnp.mean(y * y, axis=-1, keepdims=True) + NORM_EPS) * g


def _as_u32(words):
    return words if words.dtype == jnp.uint32 else pltpu.bitcast(words, jnp.uint32)


def _pack_bf16_pairs(x):
    n = x.shape[1] // 2
    return _as_u32(pltpu.pack_elementwise([x[:, :n], x[:, n:]], packed_dtype=BF16))


def _unpack_bf16_pairs(p):
    halves = [pltpu.unpack_elementwise(p, index=n, packed_dtype=BF16, unpacked_dtype=F32) for n in range(2)]
    return jnp.concatenate(halves, axis=1)


def _outproj_kernel(x_ref, ya_ref, yb_ref, ga_ref, gb_ref, wo_ref, gf_ref, wr_hi_ref, wr_lo_ref, br_ref,
                    ltri_ref, x2_ref, h2p_ref, meta_ref, cnt_ref, carry_ref):
    i = pl.program_id(0)

    @pl.when(i == 0)
    def _():
        carry_ref[...] = jnp.zeros_like(carry_ref)

    ya = _rms(ya_ref[...], ga_ref[...]).astype(BF16)
    yb = _rms(yb_ref[...], gb_ref[...]).astype(BF16)
    wa = ya_ref.shape[1]
    x2 = x_ref[...] + _bdot(ya, wo_ref[0:wa, :]) + _bdot(yb, wo_ref[wa:, :])
    x2_ref[...] = x2
    h2 = _rms(x2, gf_ref[...])
    h2p_ref[...] = _pack_bf16_pairs(h2)

    h2_hi, h2_lo = _split_bf16(h2, 2)
    logits = _dot3(h2_hi, h2_lo, wr_hi_ref, wr_lo_ref) + br_ref[...]
    tm = logits.shape[0]
    lane = lax.broadcasted_iota(jnp.int32, (tm, LANES), 1)
    lane_f = lane.astype(F32)
    work = logits
    vals, sels, idxs = [], [], []
    for _ in range(TOP_K):
        mx = jnp.max(work, axis=1, keepdims=True)
        idx = jnp.min(jnp.where(work == mx, lane_f, float(LANES)), axis=1, keepdims=True)
        sel = lane_f == idx
        vals.append(mx)
        idxs.append(idx)
        sels.append(sel)
        work = jnp.where(sel, -jnp.inf, work)
    exps = [jnp.exp(v - vals[0]) for v in vals]
    denom = exps[0] + exps[1] + exps[2] + exps[3]
    gates = [e / denom for e in exps]

    onehot = jnp.zeros((tm, LANES), F32)
    for sel in sels:
        onehot = onehot + jnp.where(sel, 1.0, 0.0)
    before = _bdot(ltri_ref[...], onehot.astype(BF16)) + carry_ref[0:1, :]
    meta = jnp.zeros((tm, LANES), F32)
    for k in range(TOP_K):
        rank = jnp.sum(jnp.where(sels[k], before, 0.0), axis=1, keepdims=True)
        meta = jnp.where(lane == k, idxs[k], meta)
        meta = jnp.where(lane == TOP_K + k, rank, meta)
        meta = jnp.where(lane == 2 * TOP_K + k, gates[k], meta)
    meta_ref[...] = meta
    total = carry_ref[0:1, :] + jnp.sum(onehot, axis=0, keepdims=True)
    carry_ref[...] = jnp.broadcast_to(total, carry_ref.shape)
    cnt_ref[...] = jnp.broadcast_to(total, cnt_ref.shape)


def _outproj(x2d, ya, yb, ga, gb, wo, gf, wr, br):
    t, d = x2d.shape
    tm = TM_ROUTE
    wa = ya.shape[1]
    ltri = (jnp.arange(tm)[:, None] > jnp.arange(tm)[None, :]).astype(BF16)
    wr_hi = wr.astype(BF16)
    wr_lo = (wr - wr_hi.astype(F32)).astype(BF16)
    tile = lambda w: pl.BlockSpec((tm, w), lambda i: (i, 0))
    whole = lambda a: pl.BlockSpec(a.shape, lambda i: (0, 0))
    return pl.pallas_call(
        _outproj_kernel,
        grid=(t // tm,),
        in_specs=[tile(d), tile(wa), tile(wa), whole(ga), whole(gb), whole(wo), whole(gf),
                  whole(wr_hi), whole(wr_lo), whole(br), whole(ltri)],
        out_specs=[tile(d), tile(d // 2), tile(LANES), pl.BlockSpec((8, LANES), lambda i: (0, 0))],
        out_shape=[jax.ShapeDtypeStruct((t, d), F32), jax.ShapeDtypeStruct((t, d // 2), jnp.uint32),
                   jax.ShapeDtypeStruct((t, LANES), F32), jax.ShapeDtypeStruct((8, LANES), F32)],
        scratch_shapes=[pltpu.VMEM((8, LANES), F32)],
        compiler_params=_cparams(("arbitrary",)),
        name="outproj",
    )(x2d, ya, yb, ga, gb, wo, gf, wr_hi, wr_lo, br, ltri)


def _rowmap_kernel(fill_lo_ref, fill_hi_ref, dest_ref, map_ref, *, n_slots, bm):
    n = dest_ref.shape[0]
    step = pl.program_id(0)

    @pl.when(step == 0)
    def _():
        def fill_range(e, _):
            def fill(j, _):
                map_ref[j] = n_slots + (j & (bm - 1))
                return 0
            return lax.fori_loop(fill_lo_ref[e], fill_hi_ref[e], fill, 0)
        lax.fori_loop(0, fill_lo_ref.shape[0], fill_range, 0)

    def body(j, _):
        map_ref[dest_ref[j]] = step * n + j
        return 0

    lax.fori_loop(0, n, body, 0, unroll=8)


def _rowmap(dest_flat, fill_lo, fill_hi, n_rows):
    n = dest_flat.shape[0]
    chunk = ROWMAP_CHUNK
    assert BM_MOE & (BM_MOE - 1) == 0
    grid_spec = pltpu.PrefetchScalarGridSpec(
        num_scalar_prefetch=2,
        grid=(n // chunk,),
        in_specs=[pl.BlockSpec((chunk,), lambda i, *_: (i,), memory_space=pltpu.SMEM)],
        out_specs=pl.BlockSpec((n_rows,), lambda i, *_: (0,), memory_space=pltpu.SMEM),
    )
    return pl.pallas_call(
        functools.partial(_rowmap_kernel, n_slots=n, bm=BM_MOE),
        grid_spec=grid_spec,
        out_shape=jax.ShapeDtypeStruct((n_rows,), jnp.int32),
        compiler_params=_cparams(("arbitrary",)),
        name="rowmap",
    )(fill_lo, fill_hi, dest_flat)


def _moe_kernel(be_ref, first_ref, nv_ref, wslot_ref, nexte_ref, sprev_ref, snext_ref, h_ref, wgu_ref, wd_ref,
                bgu_ref, bd_ref, ysg_ref, wgu_bf_ref, wd2_ref, wgu_f32_ref, wd_f32_ref, x0_ref, x1_ref,
                y0_ref, y1_ref, gsem, ssem, wsem):
    i = pl.program_id(0)
    nv = nv_ref[0]
    bm = x0_ref.shape[0]
    n_tokens = h_ref.shape[0]
    xbufs, ybufs = (x0_ref, x1_ref), (y0_ref, y1_ref)

    def gather_row(slots_ref, r, buf):
        token = slots_ref[r] & (n_tokens - 1)
        return pltpu.make_async_copy(h_ref.at[pl.ds(token, 1), :], xbufs[buf].at[pl.ds(r, 1), :], gsem.at[buf])

    def scatter_row(slots_ref, r, buf):
        return pltpu.make_async_copy(ybufs[buf].at[pl.ds(r, 1), :], ysg_ref.at[pl.ds(slots_ref[r], 1), :],
                                     ssem.at[buf])

    @pl.when(i == 0)
    def _():
        y1_ref[...] = jnp.zeros_like(y1_ref)
        spare = pltpu.make_async_copy(y1_ref, ysg_ref.at[pl.ds(ysg_ref.shape[0] - bm, bm), :], ssem.at[1])
        spare.start()
        spare.wait()
        for r in range(bm):
            gather_row(sprev_ref, r, 0).start()

    def weight_copies(expert, slot):
        return (pltpu.make_async_copy(wgu_ref.at[expert], wgu_f32_ref.at[slot], wsem.at[slot]),
                pltpu.make_async_copy(wd_ref.at[expert], wd_f32_ref.at[slot], wsem.at[slot]))

    @pl.when(i == 0)
    def _():
        for copy in weight_copies(be_ref[0], 0):
            copy.start(priority=1)

    @pl.when(first_ref[i] == 1)
    def _():
        slot = wslot_ref[i]
        for copy in weight_copies(be_ref[i], slot):
            copy.wait()
        wgu_bf_ref[...] = wgu_f32_ref[slot].astype(BF16)
        wd = wd_f32_ref[slot]
        wd2_ref[...] = _as_u32(pltpu.pack_elementwise([wd, wd], packed_dtype=BF16))

        @pl.when(nexte_ref[i] >= 0)
        def _():
            for copy in weight_copies(nexte_ref[i], 1 - slot):
                copy.start(priority=1)

    def compute(cur):
        nxt = 1 - cur
        for r in range(bm):
            gather_row(snext_ref, r, cur).wait()
        xb = _unpack_bf16_pairs(xbufs[cur][...]).astype(BF16)
        for r in range(bm):
            scatter_row(sprev_ref, r, nxt).start(priority=1)
            gather_row(snext_ref, r, nxt).start()
        gu = _bdot(xb, wgu_bf_ref[...]) + bgu_ref[0]
        even = lax.broadcasted_iota(jnp.int32, (gu.shape[0], LANES), 1) % 2 == 0
        acts = []
        for c in range(gu.shape[1] // LANES):
            blk = gu[:, c * LANES:(c + 1) * LANES]
            gate = jnp.minimum(blk, SWIGLU_LIMIT)
            glu = gate * (1.0 / (1.0 + jnp.exp(-SWIGLU_ALPHA * gate)))
            up1 = jnp.clip(blk, -SWIGLU_LIMIT, SWIGLU_LIMIT) + 1.0
            acts.append(jnp.where(even, glu * pltpu.roll(up1, LANES - 1, axis=1), 0.0).astype(BF16))
        act = jnp.concatenate(acts, axis=1)
        wd2 = pltpu.bitcast(wd2_ref[...], BF16)
        ybufs[cur][...] = _pack_bf16_pairs(_bdot(act, wd2) + bd_ref[0])

    def drain(cur):
        nxt = 1 - cur
        for r in range(bm):
            scatter_row(sprev_ref, r, nxt).start()
        for r in range(bm):
            scatter_row(sprev_ref, r, nxt).wait()
        for r in range(bm):
            gather_row(snext_ref, r, cur).wait()

    for cur in range(2):
        parity = (i % 2) == cur

        @pl.when(parity & (i >= 1) & (i <= nv))
        def _():
            for r in range(bm):
                scatter_row(sprev_ref, r, cur).wait()

        pl.when(parity & (i < nv))(functools.partial(compute, cur))
        pl.when(parity & (i == nv))(functools.partial(drain, cur))


def _moe(block_e, first, n_valid, rowmap, h2p, wgu, wd, bgu, bd):
    t, half = h2p.shape
    assert t & (t - 1) == 0, "token count must be a power of two (slot -> token uses a bit mask)"
    bm = BM_MOE
    n_blocks = rowmap.shape[0] // bm
    dff, d = wd.shape[1:]
    n_steps = block_e.shape[0]
    wslot = ((jnp.cumsum(first) - 1) % 2).astype(jnp.int32)
    step = jnp.arange(n_steps, dtype=jnp.int32)
    later_first = jnp.where((first[None, :] == 1) & (step[None, :] > step[:, None]), step[None, :], n_steps)
    next_first = jnp.min(later_first, axis=1)
    next_e = jnp.where(next_first < n_steps, block_e[jnp.minimum(next_first, n_steps - 1)], -1).astype(jnp.int32)

    b_spec = lambda a: pl.BlockSpec((1,) + a.shape[1:], lambda i, be, *_: (be[i], 0, 0))
    slots = lambda index: pl.BlockSpec((bm,), index, memory_space=pltpu.SMEM)
    tile = pltpu.VMEM((bm, half), jnp.uint32)
    grid_spec = pltpu.PrefetchScalarGridSpec(
        num_scalar_prefetch=5,
        grid=(n_blocks + 1,),
        in_specs=[slots(lambda i, *_: (jnp.maximum(i - 1, 0),)),
                  slots(lambda i, *_: (jnp.minimum(i + 1, n_blocks - 1),)),
                  pl.BlockSpec(memory_space=pl.ANY), pl.BlockSpec(memory_space=pl.ANY),
                  pl.BlockSpec(memory_space=pl.ANY), b_spec(bgu), b_spec(bd)],
        out_specs=pl.BlockSpec(memory_space=pl.ANY),
        scratch_shapes=[pltpu.VMEM((d, 2 * dff), BF16), pltpu.VMEM((dff, d), jnp.uint32),
                        pltpu.VMEM((2, d, 2 * dff), F32), pltpu.VMEM((2, dff, d), F32),
                        tile, tile, tile, tile,
                        pltpu.SemaphoreType.DMA((2,)), pltpu.SemaphoreType.DMA((2,)),
                        pltpu.SemaphoreType.DMA((2,))],
    )
    return pl.pallas_call(
        _moe_kernel,
        grid_spec=grid_spec,
        out_shape=jax.ShapeDtypeStruct((TOP_K * t + bm, half), jnp.uint32),
        compiler_params=_cparams(("arbitrary",)),
        name="moe",
    )(block_e, first, n_valid, wslot, next_e, rowmap, rowmap, h2p, wgu, wd, bgu, bd)


def _combine_kernel(x2_ref, meta_ref, g_ref, *rest):
    ys_refs, o_ref = rest[:TOP_K], rest[TOP_K]
    meta = meta_ref[...]
    y = x2_ref[...]
    for k in range(TOP_K):
        y = y + meta[:, 2 * TOP_K + k:2 * TOP_K + k + 1] * _unpack_bf16_pairs(ys_refs[k][...])
    o_ref[...] = _rms(y, g_ref[...])


def _combine(x2, meta, g, ysg):
    t, d = x2.shape
    tm = TM_COMBINE
    tiles = t // tm
    slot_spec = lambda k: pl.BlockSpec((tm, ysg.shape[1]), lambda i: (k * tiles + i, 0))
    return pl.pallas_call(
        _combine_kernel,
        grid=(tiles,),
        in_specs=[pl.BlockSpec((tm, d), lambda i: (i, 0)),
                  pl.BlockSpec((tm, LANES), lambda i: (i, 0)),
                  pl.BlockSpec((1, d), lambda i: (0, 0))] + [slot_spec(k) for k in range(TOP_K)],
        out_specs=pl.BlockSpec((tm, d), lambda i: (i, 0)),
        out_shape=jax.ShapeDtypeStruct((t, d), F32),
        compiler_params=_cparams(("arbitrary",)),
        name="combine",
    )(x2, meta, g, *([ysg] * TOP_K))


def kernel(x, attn_norm_g, w_in, b_forget, fox_out_g, sb_out_g, w_out, ffn_norm_g, w_router, b_router,
           w_gate_up, b_gate_up, w_down, b_down, final_norm_g):
    b, s, d = x.shape
    t = b * s
    fw = N_HEADS * HEAD_DIM
    x2d = x.reshape(t, d)

    o = 0
    parts = []
    for width in (fw, fw, fw, N_HEADS, fw, fw, fw):
        parts.append(w_in[:, o:o + width])
        o += width
    wqa, wka, wva, wf, wqb, wkb, wvb = parts
    scale = 1.0 / math.sqrt(HEAD_DIM)
    w_all = jnp.concatenate(
        [_pad_heads(wqa * (scale * LOG2E)), _pad_heads(wka), _pad_heads(wva),
         wqb * scale, wkb, _pad_heads(wvb)], axis=1).astype(BF16)
    wf_pad = jnp.zeros((d, LANES), F32).at[:, :N_HEADS].set(wf)
    bf_pad = jnp.zeros((1, LANES), F32).at[0, :N_HEADS].set(b_forget)

    qa, ka, va, qb, kb, vb = _inproj(x2d, attn_norm_g.reshape(1, d), w_all, wf_pad, bf_pad, s)
    ya = _attention_call(_fox_kernel, "fox", qa, ka, va, (), s, 2 * LANES, 2 * LANES,
                         scratch=(pltpu.VMEM((8, LANES), F32), pltpu.VMEM((2, BQ, BK), F32)))
    tri = (jnp.arange(SUB)[:, None] >= jnp.arange(SUB)[None, :]).astype(BF16)
    yb = _attention_call(_sb_kernel, "sb", qb, kb, vb, (tri,), s, LANES, LANES, pairs_per_step=SB_PAIRS)

    wr = jnp.zeros((d, LANES), F32).at[:, :N_EXPERTS].set(w_router)
    br = jnp.full((1, LANES), NEG_BIG, F32).at[0, :N_EXPERTS].set(b_router)
    x2, h2p, meta, cnt = _outproj(x2d, ya, yb, fox_out_g.reshape(1, fw), sb_out_g.reshape(1, fw),
                                 w_out.astype(BF16), ffn_norm_g.reshape(1, d), wr, br)

    bm = BM_MOE
    n_blocks = (t * TOP_K + N_EXPERTS * (bm - 1)) // bm
    counts = cnt[0, :N_EXPERTS].astype(jnp.int32)
    padded = (counts + bm - 1) // bm * bm
    ends = jnp.cumsum(padded)
    starts = ends - padded
    top_idx = meta[:, 0:TOP_K].astype(jnp.int32)
    rank = meta[:, TOP_K:2 * TOP_K].astype(jnp.int32)
    dest = (starts[top_idx] + rank).T.reshape(t * TOP_K)
    block_start = jnp.arange(n_blocks + 1, dtype=jnp.int32) * bm
    block_e = jnp.minimum(jnp.sum(block_start[:, None] >= ends[None, :], axis=1), N_EXPERTS - 1).astype(jnp.int32)
    first = jnp.concatenate([jnp.ones((1,), jnp.int32), (block_e[1:] != block_e[:-1]).astype(jnp.int32)])
    n_valid = (ends[-1:] // bm).astype(jnp.int32)

    n_rows = n_blocks * bm
    fill_lo = jnp.concatenate([starts + counts, ends[-1:]]).astype(jnp.int32)
    fill_hi = jnp.concatenate([ends, jnp.full((1,), n_rows, jnp.int32)]).astype(jnp.int32)
    rowmap = _rowmap(dest, fill_lo, fill_hi, n_rows)
    dff = w_down.shape[1]
    ysg = _moe(block_e, first, n_valid, rowmap, h2p, w_gate_up, w_down,
               b_gate_up.reshape(N_EXPERTS, 1, 2 * dff), b_down.reshape(N_EXPERTS, 1, d))
    out = _combine(x2, meta, final_norm_g.reshape(1, d), ysg)
    return out.reshape(b, s, d)
```
